```python
import jax, jax.numpy as jnp
from jax import lax
import numpy as np

D_MODEL = 2048
BATCH = 4
SEQ = 4096
DEPTH = 2

ROPE_THETA = 500000.0
NORM_EPS = 1e-6
Q_BLOCK = 128
D_FF = 4 * D_MODEL
N_BRANCH = 3

MLA_HEADS = 16
MLA_Q_LORA = 512
MLA_KV_LORA = 512
MLA_NOPE = 128
MLA_ROPE = 64
MLA_V = 128

DSA_HEADS = 16
DSA_KV_HEADS = 4
DSA_HEAD_DIM = 128
DSA_ROT = DSA_HEAD_DIM // 4
IDX_HEADS = 16
IDX_HEAD_DIM = 64
IDX_ROT = IDX_HEAD_DIM // 4
INDEX_TOPK = 256

CONV_CH = D_MODEL
CONV_WIDTH = 31

IN_WIDTHS = [
    MLA_Q_LORA,
    MLA_KV_LORA,
    MLA_ROPE,
    DSA_HEADS * DSA_HEAD_DIM,
    DSA_KV_HEADS * DSA_HEAD_DIM,
    DSA_KV_HEADS * DSA_HEAD_DIM,
    IDX_HEADS * IDX_HEAD_DIM,
    IDX_HEAD_DIM,
    IDX_HEADS,
    2 * CONV_CH,
    N_BRANCH * D_MODEL,
]
IN_COLS = int(sum(IN_WIDTHS))
IN_SPLITS = [int(v) for v in np.cumsum(IN_WIDTHS)[:-1]]

kernel_name = "hybrid_mla_dsa_conformer_gated"


def rms_norm(x, g):
    xf = x.astype(jnp.float32)
    y = xf * lax.rsqrt(jnp.mean(xf * xf, axis=-1, keepdims=True) + NORM_EPS)
    return (y * g.astype(jnp.float32)).astype(x.dtype)


def layer_norm(x, g, b):
    xf = x.astype(jnp.float32)
    mu = jnp.mean(xf, axis=-1, keepdims=True)
    xc = xf - mu
    var = jnp.mean(xc * xc, axis=-1, keepdims=True)
    y = xc * lax.rsqrt(var + NORM_EPS) * g.astype(jnp.float32) + b.astype(jnp.float32)
    return y.astype(x.dtype)


def rope_tables(seq_len, rot_dim):
    inv = ROPE_THETA ** (-jnp.arange(0, rot_dim, 2, dtype=jnp.float32) / rot_dim)
    ang = jnp.arange(seq_len, dtype=jnp.float32)[:, None] * inv[None, :]
    return jnp.cos(ang), jnp.sin(ang)


def apply_rotary(x, cos, sin, rot_dim):
    half = rot_dim // 2
    xr = x[..., :rot_dim].astype(jnp.float32)
    x1, x2 = xr[..., :half], xr[..., half:]
    shape = (1, cos.shape[0]) + (1,) * (x.ndim - 3) + (half,)
    c, s = cos.reshape(shape), sin.reshape(shape)
    rot = jnp.concatenate([x1 * c - x2 * s, x2 * c + x1 * s], axis=-1).astype(x.dtype)
    return jnp.concatenate([rot, x[..., rot_dim:]], axis=-1)


def to_blocks(a):
    b, t = a.shape[:2]
    a = a.reshape((b, t // Q_BLOCK, Q_BLOCK) + a.shape[2:])
    return jnp.moveaxis(a, 1, 0)


def from_blocks(a):
    a = jnp.moveaxis(a, 0, 1)
    return a.reshape((a.shape[0], a.shape[1] * a.shape[2]) + a.shape[3:])


def dense_causal_attention(q, k, v, scale):
    t = q.shape[1]
    key_pos = jnp.arange(k.shape[1])

    def one_block(args):
        qb, start = args
        s = jnp.einsum('bqhd,bshd->bhqs', qb, k).astype(jnp.float32) * scale
        qpos = start + jnp.arange(Q_BLOCK)
        mask = key_pos[None, :] <= qpos[:, None]
        s = jnp.where(mask[None, None], s, -jnp.inf)
        p = jax.nn.softmax(s, axis=-1).astype(v.dtype)
        return jnp.einsum('bhqs,bshd->bqhd', p, v)

    starts = jnp.arange(t // Q_BLOCK, dtype=jnp.int32) * Q_BLOCK
    return from_blocks(lax.map(one_block, (to_blocks(q), starts)))


def mla_branch(c_q, c_kv, k_rope_raw, g_q, g_kv, w_uq, w_ukv, cos, sin):
    b, t, _ = c_q.shape
    q = (rms_norm(c_q, g_q) @ w_uq).reshape(b, t, MLA_HEADS, MLA_NOPE + MLA_ROPE)
    q = jnp.concatenate([q[..., :MLA_NOPE],
                         apply_rotary(q[..., MLA_NOPE:], cos, sin, MLA_ROPE)], axis=-1)
    kv = (rms_norm(c_kv, g_kv) @ w_ukv).reshape(b, t, MLA_HEADS, MLA_NOPE + MLA_V)
    k_nope, v = kv[..., :MLA_NOPE], kv[..., MLA_NOPE:]
    k_rope = apply_rotary(k_rope_raw, cos, sin, MLA_ROPE)
    k = jnp.concatenate(
        [k_nope, jnp.broadcast_to(k_rope[:, :, None, :], (b, t, MLA_HEADS, MLA_ROPE))], axis=-1)
    o = dense_causal_attention(q, k, v, (MLA_NOPE + MLA_ROPE) ** -0.5)
    return o.reshape(b, t, MLA_HEADS * MLA_V)


def dsa_branch(q, k, v, q_idx, k_idx, w_idx, cos_a, sin_a, cos_i, sin_i):
    b, t, _ = q.shape
    grp = DSA_HEADS // DSA_KV_HEADS
    q = apply_rotary(q.reshape(b, t, DSA_KV_HEADS, grp, DSA_HEAD_DIM), cos_a, sin_a, DSA_ROT)
    k = apply_rotary(k.reshape(b, t, DSA_KV_HEADS, DSA_HEAD_DIM), cos_a, sin_a, DSA_ROT)
    v = v.reshape(b, t, DSA_KV_HEADS, DSA_HEAD_DIM)
    q_idx = apply_rotary(q_idx.reshape(b, t, IDX_HEADS, IDX_HEAD_DIM), cos_i, sin_i, IDX_ROT)
    k_idx = apply_rotary(k_idx, cos_i, sin_i, IDX_ROT)
    w_idx = w_idx.astype(jnp.float32) * (IDX_HEADS * IDX_HEAD_DIM) ** -0.5
    top_k = min(INDEX_TOPK, t // 4)
    key_pos = jnp.arange(t)
    gather = jax.vmap(lambda a, i: a[i])

    def one_block(args):
        qb, qib, wb, start = args
        qpos = start + jnp.arange(Q_BLOCK)
        causal = key_pos[None, :] <= qpos[:, None]
        dots = jnp.einsum('bqhd,bsd->bqhs', qib, k_idx).astype(jnp.float32)
        score = jnp.einsum('bqh,bqhs->bqs', wb, jax.nn.relu(dots))
        score = jnp.where(causal[None], score, -jnp.inf)
        _, idx = lax.top_k(score, top_k)
        valid = idx <= qpos[None, :, None]
        k_sel = gather(k, idx)
        v_sel = gather(v, idx)
        s = jnp.einsum('bqgrd,bqkgd->bqgrk', qb, k_sel).astype(jnp.float32) * DSA_HEAD_DIM ** -0.5
        s = jnp.where(valid[:, :, None, None, :], s, -jnp.inf)
        p = jax.nn.softmax(s, axis=-1).astype(v.dtype)
        return jnp.einsum('bqgrk,bqkgd->bqgrd', p, v_sel)

    starts = jnp.arange(t // Q_BLOCK, dtype=jnp.int32) * Q_BLOCK
    o = lax.map(one_block, (to_blocks(q), to_blocks(q_idx), to_blocks(w_idx), starts))
    return from_blocks(o).reshape(b, t, DSA_HEADS * DSA_HEAD_DIM)


def conv_branch(h, w_dw, b_dw, g_ln, b_ln):
    a, gate = jnp.split(h, 2, axis=-1)
    y = a * jax.nn.sigmoid(gate)
    y = lax.conv_general_dilated(
        y, w_dw[:, None, :].astype(y.dtype), window_strides=(1,),
        padding=[(CONV_WIDTH - 1, 0)],
        dimension_numbers=('NWC', 'WIO', 'NWC'),
        feature_group_count=CONV_CH) + b_dw
    return jax.nn.silu(layer_norm(y, g_ln, b_ln))


def hybrid_mixer(u, w_in, mla_q_norm, mla_kv_norm, mla_w_uq, mla_w_ukv,
                 conv_w_dw, conv_b_dw, conv_ln_g, conv_ln_b,
                 w_o_mla, w_o_dsa, w_o_conv, w_out, ropes):
    b, t, _ = u.shape
    (c_q, c_kv, k_rope, q_a, k_a, v_a, q_i, k_i, w_i, conv_in, gates) = jnp.split(
        u @ w_in, IN_SPLITS, axis=-1)
    cos_m, sin_m, cos_a, sin_a, cos_i, sin_i = ropes
    y_a = mla_branch(c_q, c_kv, k_rope, mla_q_norm, mla_kv_norm, mla_w_uq, mla_w_ukv,
                     cos_m, sin_m) @ w_o_mla
    y_b = dsa_branch(q_a, k_a, v_a, q_i, k_i, w_i, cos_a, sin_a, cos_i, sin_i) @ w_o_dsa
    y_c = conv_branch(conv_in, conv_w_dw, conv_b_dw, conv_ln_g, conv_ln_b) @ w_o_conv
    g = jax.nn.sigmoid(gates.astype(jnp.float32)).astype(u.dtype).reshape(b, t, N_BRANCH, D_MODEL)
    merged = g[:, :, 0] * y_a + g[:, :, 1] * y_b + g[:, :, 2] * y_c
    return merged @ w_out


def squared_relu_mlp(u, w_up, w_down):
    h = jax.nn.relu(u @ w_up)
    return (h * h) @ w_down


def setup_inputs(seed: int = 0) -> dict:
    key = jax.random.key(seed)
    ks = jax.random.split(key, 24)
    f32 = jnp.float32

    def dense(k, shape, fan_in):
        return jax.random.normal(k, shape, f32) * fan_in ** -0.5

    def gain(k, shape):
        return 1.0 + 0.02 * jax.random.normal(k, shape, f32)

    def small(k, shape):
        return 0.02 * jax.random.normal(k, shape, f32)

    L = DEPTH
    return {
        "x": jax.random.normal(ks[0], (BATCH, SEQ, D_MODEL), f32),
        "attn_norm": gain(ks[1], (L, D_MODEL)),
        "w_in": dense(ks[2], (L, D_MODEL, IN_COLS), D_MODEL),
        "mla_q_norm": gain(ks[3], (L, MLA_Q_LORA)),
        "mla_kv_norm": gain(ks[4], (L, MLA_KV_LORA)),
        "mla_w_uq": dense(ks[5], (L, MLA_Q_LORA, MLA_HEADS * (MLA_NOPE + MLA_ROPE)), MLA_Q_LORA),
        "mla_w_ukv": dense(ks[6], (L, MLA_KV_LORA, MLA_HEADS * (MLA_NOPE + MLA_V)), MLA_KV_LORA),
        "conv_w_dw": dense(ks[7], (L, CONV_WIDTH, CONV_CH), CONV_WIDTH),
        "conv_b_dw": small(ks[8], (L, CONV_CH)),
        "conv_ln_g": gain(ks[9], (L, CONV_CH)),
        "conv_ln_b": small(ks[10], (L, CONV_CH)),
        "w_o_mla": dense(ks[11], (L, MLA_HEADS * MLA_V, D_MODEL), MLA_HEADS * MLA_V),
        "w_o_dsa": dense(ks[12], (L, DSA_HEADS * DSA_HEAD_DIM, D_MODEL), DSA_HEADS * DSA_HEAD_DIM),
        "w_o_conv": dense(ks[13], (L, CONV_CH, D_MODEL), CONV_CH),
        "w_out": dense(ks[14], (L, D_MODEL, D_MODEL), D_MODEL),
        "mlp_norm": gain(ks[15], (L, D_MODEL)),
        "w_up": dense(ks[16], (L, D_MODEL, D_FF), D_MODEL),
        "w_down": dense(ks[17], (L, D_FF, D_MODEL), 2 * D_FF),
        "final_norm": gain(ks[18], (D_MODEL,)),
    }


def reference(x, attn_norm, w_in, mla_q_norm, mla_kv_norm, mla_w_uq, mla_w_ukv,
              conv_w_dw, conv_b_dw, conv_ln_g, conv_ln_b,
              w_o_mla, w_o_dsa, w_o_conv, w_out, mlp_norm, w_up, w_down, final_norm):
    t = x.shape[1]
    cos_m, sin_m = rope_tables(t, MLA_ROPE)
    cos_a, sin_a = rope_tables(t, DSA_ROT)
    cos_i, sin_i = rope_tables(t, IDX_ROT)
    ropes = (cos_m, sin_m, cos_a, sin_a, cos_i, sin_i)
    for i in range(DEPTH):
        u = rms_norm(x, attn_norm[i])
        x = x + hybrid_mixer(u, w_in[i], mla_q_norm[i], mla_kv_norm[i], mla_w_uq[i], mla_w_ukv[i],
                             conv_w_dw[i], conv_b_dw[i], conv_ln_g[i], conv_ln_b[i],
                             w_o_mla[i], w_o_dsa[i], w_o_conv[i], w_out[i], ropes)
        x = x + squared_relu_mlp(rms_norm(x, mlp_norm[i]), w_up[i], w_down[i])
    return rms_norm(x, final_norm)
```

```python
import functools

import jax
import jax.numpy as jnp
import numpy as np
from jax import lax
from jax.experimental import pallas as pl
from jax.experimental.pallas import tpu as pltpu

F32 = jnp.float32
BF16 = jnp.bfloat16

ROPE_THETA = 500000.0
NORM_EPS = 1e-6
N_BRANCH = 3

MLA_HEADS = 16
MLA_Q_LORA = 512
MLA_KV_LORA = 512
MLA_NOPE = 128
MLA_ROPE = 64
MLA_V = 128

DSA_HEADS = 16
DSA_KV_HEADS = 4
DSA_HEAD_DIM = 128
DSA_ROT = DSA_HEAD_DIM // 4
IDX_HEADS = 16
IDX_HEAD_DIM = 64
IDX_ROT = IDX_HEAD_DIM // 4
INDEX_TOPK = 256

CONV_WIDTH = 31

LANES = 128
COL_TILE = 512
CONV_HALO = 32
MASK_VALUE = -1e30
INT_MIN = -2 ** 31
VMEM_LIMIT = 56 * 1024 * 1024

_NT = (((1,), (1,)), ((), ()))


def _cparams(*sem):
    return pltpu.CompilerParams(dimension_semantics=sem, vmem_limit_bytes=VMEM_LIMIT)


def _in_layout(d_model):
    c = d_model
    off = {"q_a": 0, "q_i": 2048, "c_q": 3072, "c_kv": 3584, "conv_a": 4096, "conv_g": 4096 + c,
           "gate": 4096 + 2 * c}
    off["k_a"] = off["gate"] + N_BRANCH * d_model
    off["v_a"] = off["k_a"] + 512
    off["small"] = off["v_a"] + 512
    off["total"] = off["small"] + 512
    return off


def _rope_lane_tables(seq_len, rot_dim, lane_starts):
    half = rot_dim // 2
    inv = ROPE_THETA ** (-jnp.arange(0, rot_dim, 2, dtype=F32) / rot_dim)
    ang = jnp.arange(seq_len, dtype=F32)[:, None] * inv[None, :]
    cos, sin = jnp.cos(ang), jnp.sin(ang)
    ones = lambda n: jnp.ones((seq_len, n), F32)
    zeros = lambda n: jnp.zeros((seq_len, n), F32)
    c_parts, a_parts, b_parts = [], [], []
    pos = 0
    for s0 in lane_starts:
        gap = s0 - pos
        c_parts += [ones(gap), cos, cos]
        a_parts += [zeros(gap), -sin, zeros(half)]
        b_parts += [zeros(gap), zeros(half), sin]
        pos = s0 + 2 * half
    c_parts.append(ones(LANES - pos))
    a_parts.append(zeros(LANES - pos))
    b_parts.append(zeros(LANES - pos))
    cat = lambda p: jnp.concatenate(p, axis=1)
    return (cat(c_parts), cat(a_parts), cat(b_parts)), half


def _rms(x, gain):
    ms = jnp.mean(x * x, axis=-1, keepdims=True)
    return x * lax.rsqrt(ms + NORM_EPS) * gain


def _proj_kernel(*refs, norm, has_rms, rope_halves, kinds, hilo_scale):
    it = iter(refs)
    a_ref = next(it)
    ga_ref = next(it) if norm else None
    w_ref = next(it)
    rg_ref = next(it) if has_rms else None
    tabs = [(next(it), next(it), next(it)) for _ in rope_halves]
    o_ref = next(it)
    u_ref = next(it) if norm else None
    j = pl.program_id(1)

    if norm:
        @pl.when(j == 0)
        def _():
            u_ref[...] = _rms(a_ref[...], ga_ref[...]).astype(BF16)
        a = u_ref[...]
    else:
        a = a_ref[...]
    acc = jnp.dot(a, w_ref[...], preferred_element_type=F32)

    def chunk_epilogue(op, x):
        if op[0] == "plain":
            return x
        if op[0] == "sigmoid":
            return jax.nn.sigmoid(x)
        if op[0] == "rope":
            c_ref, sa_ref, sb_ref = tabs[op[1]]
            half = rope_halves[op[1]]
            return (x * c_ref[...] + pltpu.roll(x, LANES - half, 1) * sa_ref[...]
                    + pltpu.roll(x, half, 1) * sb_ref[...])
        if op[0] == "hilo":
            v = x * hilo_scale
            lo = v - v.astype(o_ref.dtype).astype(F32)
            lane = lax.broadcasted_iota(jnp.int32, x.shape, 1)
            return jnp.where(lane < IDX_HEADS, v, lo)
        raise ValueError(op)

    def tile_epilogue(kind):
        if kind == "rms":
            o_ref[...] = _rms(acc, rg_ref[...]).astype(o_ref.dtype)
            return
        for c, op in enumerate(kind):
            sl = slice(c * LANES, (c + 1) * LANES)
            o_ref[:, sl] = chunk_epilogue(op, acc[:, sl]).astype(o_ref.dtype)

    if len(kinds) == 1:
        tile_epilogue(kinds[0][0])
    else:
        for kind, ranges in kinds:
            cond = None
            for lo, hi in ranges:
                c = (j >= lo) & (j < hi)
                cond = c if cond is None else (cond | c)
            pl.when(cond)(functools.partial(tile_epilogue, kind))


def _project(a, w, *, a_col_block=0, gain=None, rms_gain=None, rope_tabs=(), rope_halves=(), kinds,
             seq_len, tm, out_dtype=BF16, hilo_scale=1.0):
    n_rows = a.shape[0]
    k_dim, n_cols = w.shape
    tn = COL_TILE
    norm = gain is not None
    has_rms = rms_gain is not None
    t_blocks = seq_len // tm
    in_specs = [pl.BlockSpec((tm, k_dim), lambda i, j: (i, a_col_block))]
    args = [a]
    if norm:
        in_specs.append(pl.BlockSpec((1, k_dim), lambda i, j: (0, 0)))
        args.append(gain.reshape(1, k_dim).astype(F32))
    in_specs.append(pl.BlockSpec((k_dim, tn), lambda i, j: (0, j)))
    args.append(w)
    if has_rms:
        in_specs.append(pl.BlockSpec((1, tn), lambda i, j: (0, j)))
        args.append(rms_gain)
    for tab in rope_tabs:
        for t in tab:
            in_specs.append(pl.BlockSpec((tm, LANES), lambda i, j: (i % t_blocks, 0)))
            args.append(t)
    scratch = [pltpu.VMEM((tm, k_dim), BF16)] if norm else []
    return pl.pallas_call(
        functools.partial(_proj_kernel, norm=norm, has_rms=has_rms, rope_halves=tuple(rope_halves),
                          kinds=kinds, hilo_scale=hilo_scale),
        grid=(n_rows // tm, n_cols // tn),
        in_specs=in_specs,
        out_specs=pl.BlockSpec((tm, tn), lambda i, j: (i, j)),
        out_shape=jax.ShapeDtypeStruct((n_rows, n_cols), out_dtype),
        scratch_shapes=scratch,
        compiler_params=_cparams("parallel", "arbitrary"),
    )(*args)


def _fma_mm_kernel(*refs, has_mul, has_add):
    it = iter(refs)
    a_ref, w_ref = next(it), next(it)
    mul_ref = next(it) if has_mul else None
    add_ref = next(it) if has_add else None
    o_ref = next(it)
    acc = jnp.dot(a_ref[...], w_ref[...], preferred_element_type=F32)
    if has_mul:
        acc = acc * mul_ref[...].astype(F32)
    if has_add:
        acc = acc + add_ref[...].astype(F32)
    o_ref[...] = acc.astype(o_ref.dtype)


def _fma_matmul(a, w, *, mul=None, mul_col_block=0, add=None, tm, tn, out_dtype):
    n_rows, k_dim = a.shape
    n_cols = w.shape[1]
    nj = n_cols // tn
    in_specs = [pl.BlockSpec((tm, k_dim), lambda i, j: (i, 0)),
                pl.BlockSpec((k_dim, tn), lambda i, j: (0, j))]
    args = [a, w]
    if mul is not None:
        in_specs.append(pl.BlockSpec((tm, tn), lambda i, j: (i, mul_col_block * nj + j)))
        args.append(mul)
    if add is not None:
        in_specs.append(pl.BlockSpec((tm, tn), lambda i, j: (i, j)))
        args.append(add)
    return pl.pallas_call(
        functools.partial(_fma_mm_kernel, has_mul=mul is not None, has_add=add is not None),
        grid=(n_rows // tm, nj),
        in_specs=in_specs,
        out_specs=pl.BlockSpec((tm, tn), lambda i, j: (i, j)),
        out_shape=jax.ShapeDtypeStruct((n_rows, n_cols), out_dtype),
        compiler_params=_cparams("parallel", "arbitrary"),
    )(*args)


def _mla_flash_kernel(q_ref, kn_ref, v_ref, kr_ref, o_ref, kcat_ref, *, tq, scale):
    qi = pl.program_id(2)

    @pl.when(qi == 0)
    def _():
        kcat_ref[:, :LANES] = kn_ref[...]
        kcat_ref[:, LANES:] = kr_ref[...]

    q = q_ref[...]

    def block(j, carry, masked):
        m, l, acc = carry
        off = pl.multiple_of(j * tq, tq)
        k = kcat_ref[pl.ds(off, tq), :]
        s = lax.dot_general(q, k, _NT, preferred_element_type=F32) * scale
        if masked:
            row = lax.broadcasted_iota(jnp.int32, s.shape, 0)
            col = lax.broadcasted_iota(jnp.int32, s.shape, 1)
            s = jnp.where(col <= row, s, MASK_VALUE)
        m_new = jnp.maximum(m, jnp.max(s, axis=-1, keepdims=True))
        alpha = jnp.exp(m - m_new)
        p = jnp.exp(s - m_new)
        l = alpha * l + jnp.sum(p, axis=-1, keepdims=True)
        acc = alpha * acc + jnp.dot(p.astype(BF16), v_ref[pl.ds(off, tq), :], preferred_element_type=F32)
        return m_new, l, acc

    init = (jnp.full((tq, 1), MASK_VALUE, F32), jnp.zeros((tq, 1), F32), jnp.zeros((tq, MLA_V), F32))
    carry = lax.fori_loop(0, qi, functools.partial(block, masked=False), init)
    _, l, acc = block(qi, carry, True)
    o_ref[...] = (acc / l).astype(o_ref.dtype)


def _mla_attention(q, kv, big, *, batch, seq_len, kr_col_block, tq):
    n_rows = batch * seq_len
    nq = seq_len // tq
    return pl.pallas_call(
        functools.partial(_mla_flash_kernel, tq=tq, scale=float((MLA_NOPE + MLA_ROPE) ** -0.5)),
        grid=(batch, MLA_HEADS, nq),
        in_specs=[
            pl.BlockSpec((tq, 2 * LANES), lambda b, h, i: (b * nq + i, h)),
            pl.BlockSpec((seq_len, LANES), lambda b, h, i: (b, h)),
            pl.BlockSpec((seq_len, LANES), lambda b, h, i: (b, MLA_HEADS + h)),
            pl.BlockSpec((seq_len, LANES), lambda b, h, i: (b, kr_col_block)),
        ],
        out_specs=pl.BlockSpec((tq, MLA_V), lambda b, h, i: (b * nq + i, h)),
        out_shape=jax.ShapeDtypeStruct((n_rows, MLA_HEADS * MLA_V), BF16),
        scratch_shapes=[pltpu.VMEM((seq_len, 2 * LANES), BF16)],
        compiler_params=_cparams("parallel", "parallel", "arbitrary"),
    )(q, kv, kv, big)


def _dsa_kernel(qa_ref, qi_ref, w_ref, ka_ref, va_ref, ke_ref, ko_ref, o_ref,
                key_ref, bias_ref, wrep_ref, *, tq, tk, top_k, seq_bits, scale):
    i = pl.program_id(1)
    n_chunks = (i * tq + tq + tk - 1) // tk
    n_sub = tk // LANES
    row = i * tq + lax.broadcasted_iota(jnp.int32, (tq, LANES), 0)
    lane = lax.broadcasted_iota(jnp.int32, (tq, LANES), 1)

    wf = w_ref[...].astype(F32)
    wsum = wf + pltpu.roll(wf, LANES - IDX_HEADS, 1)
    for h in range(IDX_HEADS):
        wrep_ref[h] = jnp.broadcast_to(wsum[:, h:h + 1], (tq, LANES))

    def score_chunk(c, _):
        off = pl.multiple_of(c * tk, tk)
        ke = ke_ref[pl.ds(off, tk), :]
        ko = ko_ref[pl.ds(off, tk), :]
        sc = [jnp.zeros((tq, LANES), F32) for _ in range(n_sub)]
        for ch in range(IDX_HEADS // 2):
            qc = qi_ref[:, ch * LANES:(ch + 1) * LANES]
            de = lax.dot_general(qc, ke, _NT, preferred_element_type=F32)
            do = lax.dot_general(qc, ko, _NT, preferred_element_type=F32)
            we, wo = wrep_ref[2 * ch], wrep_ref[2 * ch + 1]
            for s in range(n_sub):
                sl = slice(s * LANES, (s + 1) * LANES)
                sc[s] = sc[s] + we * jnp.maximum(de[:, sl], 0.0) + wo * jnp.maximum(do[:, sl], 0.0)
        for s in range(n_sub):
            v = jnp.where(sc[s] == 0.0, 0.0, sc[s])
            bits = pltpu.bitcast(v, jnp.int32)
            key = jnp.where(bits < 0, bits ^ jnp.int32(0x7FFFFFFF), bits)
            col = off + s * LANES + lane
            key_ref[c, :, s * LANES:(s + 1) * LANES] = jnp.where(col <= row, key, jnp.int32(INT_MIN))
        return 0

    lax.fori_loop(0, n_chunks, score_chunk, 0)

    def count(pred):
        def body(c, cnt):
            for s in range(n_sub):
                key = key_ref[c, :, s * LANES:(s + 1) * LANES]
                col = c * tk + s * LANES + lane
                cnt = cnt + jnp.where(pred(key, col), 1, 0)
            return cnt
        cnt = lax.fori_loop(0, n_chunks, body, jnp.zeros((tq, LANES), jnp.int32))
        return jnp.broadcast_to(jnp.sum(cnt, axis=1, keepdims=True), (tq, LANES))

    zero = jnp.zeros((tq, LANES), jnp.int32)
    thr = jnp.where(count(lambda key, col: key >= zero) >= top_k, zero, jnp.int32(INT_MIN))

    def search(t, thr):
        cand = thr | (jnp.int32(1) << (30 - t))
        return jnp.where(count(lambda key, col: key >= cand) >= top_k, cand, thr)

    thr = lax.fori_loop(0, 31, search, thr)
    thr = jnp.maximum(thr, jnp.int32(INT_MIN + 1))

    need = top_k - count(lambda key, col: key > thr)
    n_eq = count(lambda key, col: key == thr)
    has_excess = jnp.max(jnp.where(n_eq > need, 1, 0)) > 0

    def tie_cut():
        def step(t, cut):
            cand = cut | (jnp.int32(1) << (seq_bits - 1 - t))
            cnt = count(lambda key, col: (key == thr) & (col < cand))
            return jnp.where(cnt < need, cand, cut)
        return lax.fori_loop(0, seq_bits, step, zero)

    cut = lax.cond(has_excess, tie_cut, lambda: jnp.full((tq, LANES), 2 ** seq_bits, jnp.int32))

    def bias_chunk(c, _):
        for s in range(n_sub):
            key = key_ref[c, :, s * LANES:(s + 1) * LANES]
            col = c * tk + s * LANES + lane
            sel = (key > thr) | ((key == thr) & (col <= cut))
            bias_ref[c, :, s * LANES:(s + 1) * LANES] = jnp.where(sel, 0.0, MASK_VALUE)
        return 0

    lax.fori_loop(0, n_chunks, bias_chunk, 0)

    grp = DSA_HEADS // DSA_KV_HEADS
    for g in range(DSA_KV_HEADS):
        gs = slice(g * DSA_HEAD_DIM, (g + 1) * DSA_HEAD_DIM)

        def attend(c, carry, gs=gs, g=g):
            off = pl.multiple_of(c * tk, tk)
            k = ka_ref[pl.ds(off, tk), gs]
            v = va_ref[pl.ds(off, tk), gs]
            bias = bias_ref[c]
            out = []
            for r in range(grp):
                m, l, acc = carry[r]
                h = g * grp + r
                q = qa_ref[:, h * DSA_HEAD_DIM:(h + 1) * DSA_HEAD_DIM]
                s = lax.dot_general(q, k, _NT, preferred_element_type=F32) * scale + bias
                m_new = jnp.maximum(m, jnp.max(s, axis=-1, keepdims=True))
                alpha = jnp.exp(m - m_new)
                p = jnp.exp(s - m_new)
                l = alpha * l + jnp.sum(p, axis=-1, keepdims=True)
                acc = alpha * acc + jnp.dot(p.astype(BF16), v, preferred_element_type=F32)
                out.append((m_new, l, acc))
            return tuple(out)

        init = tuple((jnp.full((tq, 1), MASK_VALUE, F32), jnp.zeros((tq, 1), F32),
                      jnp.zeros((tq, DSA_HEAD_DIM), F32)) for _ in range(grp))
        res = lax.fori_loop(0, n_chunks, attend, init)
        for r in range(grp):
            _, l, acc = res[r]
            h = g * grp + r
            o_ref[:, h * DSA_HEAD_DIM:(h + 1) * DSA_HEAD_DIM] = (acc / l).astype(o_ref.dtype)


def _dsa_attention(big, *, batch, seq_len, off, tq, tk):
    n_rows = batch * seq_len
    nq = seq_len // tq
    top_k = min(INDEX_TOPK, seq_len // 4)
    seq_bits = int(seq_len).bit_length() - 1
    assert 2 ** seq_bits == seq_len and tk >= top_k
    qa_w, qi_w, kv_w = DSA_HEADS * DSA_HEAD_DIM, IDX_HEADS * IDX_HEAD_DIM, DSA_KV_HEADS * DSA_HEAD_DIM
    small = off["small"] // LANES
    rowmap = lambda col: (lambda b, i: (b * nq + i, col))
    seqmap = lambda col: (lambda b, i: (b, col))
    n_ck = seq_len // tk
    return pl.pallas_call(
        functools.partial(_dsa_kernel, tq=tq, tk=tk, top_k=top_k, seq_bits=seq_bits,
                          scale=float(DSA_HEAD_DIM ** -0.5)),
        grid=(batch, nq),
        in_specs=[
            pl.BlockSpec((tq, qa_w), rowmap(off["q_a"] // qa_w)),
            pl.BlockSpec((tq, qi_w), rowmap(off["q_i"] // qi_w)),
            pl.BlockSpec((tq, LANES), rowmap(small + 3)),
            pl.BlockSpec((seq_len, kv_w), seqmap(off["k_a"] // kv_w)),
            pl.BlockSpec((seq_len, kv_w), seqmap(off["v_a"] // kv_w)),
            pl.BlockSpec((seq_len, LANES), seqmap(small + 1)),
            pl.BlockSpec((seq_len, LANES), seqmap(small + 2)),
        ],
        out_specs=pl.BlockSpec((tq, qa_w), lambda b, i: (b * nq + i, 0)),
        out_shape=jax.ShapeDtypeStruct((n_rows, qa_w), BF16),
        scratch_shapes=[pltpu.VMEM((n_ck, tq, tk), jnp.int32), pltpu.VMEM((n_ck, tq, tk), F32),
                        pltpu.VMEM((IDX_HEADS, tq, LANES), F32)],
        compiler_params=_cparams("parallel", "arbitrary"),
    )(big, big, big, big, big, big, big)


def _conv_kernel(a_ref, g_ref, ha_ref, hg_ref, gate_ref, wdw_ref, bdw_ref, lng_ref, lnb_ref, wo_ref,
                 o_ref, y_ref, c_ref, *, tt, rows, cols):
    t = pl.program_id(1)
    y_ref[CONV_HALO:, :] = a_ref[...].astype(F32) * jax.nn.sigmoid(g_ref[...].astype(F32))
    halo = ha_ref[...].astype(F32) * jax.nn.sigmoid(hg_ref[...].astype(F32))
    y_ref[:CONV_HALO, :] = jnp.where(t == 0, 0.0, halo)

    n_ch = y_ref.shape[1]
    lead = CONV_HALO - (CONV_WIDTH - 1)
    for c0 in range(0, n_ch, cols):
        cs = slice(c0, c0 + cols)
        for r0 in range(0, tt, rows):
            acc = jnp.broadcast_to(bdw_ref[:, cs], (rows, cols))
            for j in range(CONV_WIDTH):
                acc = acc + wdw_ref[j:j + 1, cs] * y_ref[r0 + lead + j:r0 + lead + j + rows, cs]
            c_ref[r0:r0 + rows, cs] = acc

    x = c_ref[...]
    mu = jnp.mean(x, axis=-1, keepdims=True)
    xc = x - mu
    var = jnp.mean(xc * xc, axis=-1, keepdims=True)
    yn = xc * lax.rsqrt(var + NORM_EPS) * lng_ref[...] + lnb_ref[...]
    z = yn * jax.nn.sigmoid(yn)
    proj = jnp.dot(z.astype(BF16), wo_ref[...], preferred_element_type=F32)
    o_ref[...] = (gate_ref[...].astype(F32) * proj).astype(o_ref.dtype)


def _conv_branch(big, w_dw, b_dw, ln_g, ln_b, w_o, *, batch, seq_len, off, d_model, tt):
    n_rows = batch * seq_len
    n_ch = w_dw.shape[1]
    nt = seq_len // tt
    hb = tt // CONV_HALO
    w_pad = jnp.concatenate([w_dw.astype(F32), jnp.zeros((CONV_HALO - CONV_WIDTH, n_ch), F32)], axis=0)
    ca, cg, gc = off["conv_a"] // n_ch, off["conv_g"] // n_ch, off["gate"] // d_model + 2
    cur = lambda col: (lambda b, t: (b * nt + t, col))
    halo = lambda col: (lambda b, t: (jnp.maximum((b * nt + t) * hb - 1, 0), col))
    const = lambda b, t: (0, 0)
    vec = lambda v: v.reshape(1, n_ch).astype(F32)
    return pl.pallas_call(
        functools.partial(_conv_kernel, tt=tt, rows=32, cols=512),
        grid=(batch, nt),
        in_specs=[
            pl.BlockSpec((tt, n_ch), cur(ca)),
            pl.BlockSpec((tt, n_ch), cur(cg)),
            pl.BlockSpec((CONV_HALO, n_ch), halo(ca)),
            pl.BlockSpec((CONV_HALO, n_ch), halo(cg)),
            pl.BlockSpec((tt, d_model), cur(gc)),
            pl.BlockSpec((CONV_HALO, n_ch), const),
            pl.BlockSpec((1, n_ch), const),
            pl.BlockSpec((1, n_ch), const),
            pl.BlockSpec((1, n_ch), const),
            pl.BlockSpec((n_ch, d_model), const),
        ],
        out_specs=pl.BlockSpec((tt, d_model), lambda b, t: (b * nt + t, 0)),
        out_shape=jax.ShapeDtypeStruct((n_rows, d_model), F32),
        scratch_shapes=[pltpu.VMEM((tt + CONV_HALO, n_ch), F32), pltpu.VMEM((tt, n_ch), F32)],
        compiler_params=_cparams("parallel", "arbitrary"),
    )(big, big, big, big, big, w_pad, vec(b_dw), vec(ln_g), vec(ln_b), w_o)


def _mlp_kernel(x_ref, g_ref, wu_ref, wd_ref, gf_ref, o_ref, u_ref, acc_ref, *, final_norm):
    f = pl.program_id(1)

    @pl.when(f == 0)
    def _():
        u_ref[...] = _rms(x_ref[...], g_ref[...]).astype(BF16)
        acc_ref[...] = jnp.zeros_like(acc_ref)

    h = jnp.maximum(jnp.dot(u_ref[...], wu_ref[...], preferred_element_type=F32), 0.0)
    acc_ref[...] += jnp.dot((h * h).astype(BF16), wd_ref[...], preferred_element_type=F32)

    @pl.when(f == pl.num_programs(1) - 1)
    def _():
        y = x_ref[...] + acc_ref[...]
        if final_norm:
            y = _rms(y, gf_ref[...])
        o_ref[...] = y


def _mlp(x, gain, w_up, w_down, final_gain, *, final_norm, tm, tf):
    n_rows, d_model = x.shape
    d_ff = w_up.shape[1]
    vec = lambda v: v.reshape(1, d_model).astype(F32)
    return pl.pallas_call(
        functools.partial(_mlp_kernel, final_norm=final_norm),
        grid=(n_rows // tm, d_ff // tf),
        in_specs=[
            pl.BlockSpec((tm, d_model), lambda i, f: (i, 0)),
            pl.BlockSpec((1, d_model), lambda i, f: (0, 0)),
            pl.BlockSpec((d_model, tf), lambda i, f: (0, f)),
            pl.BlockSpec((tf, d_model), lambda i, f: (f, 0)),
            pl.BlockSpec((1, d_model), lambda i, f: (0, 0)),
        ],
        out_specs=pl.BlockSpec((tm, d_model), lambda i, f: (i, 0)),
        out_shape=jax.ShapeDtypeStruct((n_rows, d_model), F32),
        scratch_shapes=[pltpu.VMEM((tm, d_model), BF16), pltpu.VMEM((tm, d_model), F32)],
        compiler_params=_cparams("parallel", "arbitrary"),
    )(x, vec(gain), w_up, w_down, vec(final_gain))


def _prep_w_in(w_in, d_model):
    widths = [MLA_Q_LORA, MLA_KV_LORA, MLA_ROPE, DSA_HEADS * DSA_HEAD_DIM, DSA_KV_HEADS * DSA_HEAD_DIM,
              DSA_KV_HEADS * DSA_HEAD_DIM, IDX_HEADS * IDX_HEAD_DIM, IDX_HEAD_DIM, IDX_HEADS,
              2 * d_model, N_BRANCH * d_model]
    splits = [int(v) for v in np.cumsum(widths)[:-1]]
    c_q, c_kv, k_rope, q_a, k_a, v_a, q_i, k_i, w_i, conv_in, gates = jnp.split(w_in, splits, axis=1)
    z = lambda n: jnp.zeros((w_in.shape[0], n), w_in.dtype)
    small = jnp.concatenate([k_rope, z(64), k_i, z(64), z(64), k_i, w_i, w_i, z(LANES - 2 * IDX_HEADS)], axis=1)
    return jnp.concatenate([q_a, q_i, c_q, c_kv, conv_in, gates, k_a, v_a, small], axis=1).astype(BF16)


def _prep_w_uq(w_uq):
    k = w_uq.shape[0]
    w = w_uq.reshape(k, MLA_HEADS, MLA_NOPE + MLA_ROPE)
    w = jnp.concatenate([w, jnp.zeros((k, MLA_HEADS, 2 * LANES - MLA_NOPE - MLA_ROPE), w.dtype)], axis=2)
    return w.reshape(k, MLA_HEADS * 2 * LANES).astype(BF16)


def _prep_w_ukv(w_ukv):
    k = w_ukv.shape[0]
    w = w_ukv.reshape(k, MLA_HEADS, 2, MLA_NOPE)
    return jnp.transpose(w, (0, 2, 1, 3)).reshape(k, 2 * MLA_HEADS * MLA_NOPE).astype(BF16)


def _tile_ranges(off, d_model):
    t = lambda name: off[name] // COL_TILE
    rope_dsa = (("rope", 1),) * 4
    rope_idx = (("rope", 2),) * 4
    plain = (("plain",),) * 4
    sigm = (("sigmoid",),) * 4
    small = (("rope", 0), ("rope", 2), ("rope", 2), ("hilo",))
    return (
        (rope_dsa, ((t("q_a"), t("q_i")), (t("k_a"), t("v_a")))),
        (rope_idx, ((t("q_i"), t("c_q")),)),
        ("rms", ((t("c_q"), t("conv_a")),)),
        (plain, ((t("conv_a"), t("gate")), (t("v_a"), t("small")))),
        (sigm, ((t("gate"), t("k_a")),)),
        (small, ((t("small"), t("total")),)),
    )


def _pick(n, pref):
    return pref if n % pref == 0 else n


def kernel(x, attn_norm, w_in, mla_q_norm, mla_kv_norm, mla_w_uq, mla_w_ukv, conv_w_dw, conv_b_dw,
           conv_ln_g, conv_ln_b, w_o_mla, w_o_dsa, w_o_conv, w_out, mlp_norm, w_up, w_down, final_norm):
    batch, seq_len, d_model = x.shape
    depth = w_in.shape[0]
    n_rows = batch * seq_len
    off = _in_layout(d_model)
    kinds = _tile_ranges(off, d_model)

    tab_mla, half_mla = _rope_lane_tables(seq_len, MLA_ROPE, (0,))
    tab_dsa, half_dsa = _rope_lane_tables(seq_len, DSA_ROT, (0,))
    tab_idx, half_idx = _rope_lane_tables(seq_len, IDX_ROT, (0, IDX_HEAD_DIM))
    tabs = (tab_mla, tab_dsa, tab_idx)
    halves = (half_mla, half_dsa, half_idx)

    tm = _pick(seq_len, 512)
    xf = x.reshape(n_rows, d_model)
    for i in range(depth):
        ones = jnp.ones((off["total"],), F32)
        rms_gain = lax.dynamic_update_slice(ones, mla_q_norm[i].astype(F32), (off["c_q"],))
        rms_gain = lax.dynamic_update_slice(rms_gain, mla_kv_norm[i].astype(F32), (off["c_kv"],))
        big = _project(xf, _prep_w_in(w_in[i], d_model), gain=attn_norm[i], rms_gain=rms_gain.reshape(1, -1),
                       rope_tabs=tabs, rope_halves=halves, kinds=kinds, seq_len=seq_len, tm=tm,
                       hilo_scale=float((IDX_HEADS * IDX_HEAD_DIM) ** -0.5))

        q_mla = _project(big, _prep_w_uq(mla_w_uq[i]), a_col_block=off["c_q"] // MLA_Q_LORA,
                         rope_tabs=(tab_mla,), rope_halves=(half_mla,),
                         kinds=(((("plain",), ("rope", 0)) * 2, ((0, 1 << 30),)),), seq_len=seq_len, tm=tm)
        kv_mla = _project(big, _prep_w_ukv(mla_w_ukv[i]), a_col_block=off["c_kv"] // MLA_KV_LORA,
                          kinds=(((("plain",),) * 4, ((0, 1 << 30),)),), seq_len=seq_len, tm=tm)
        o_mla = _mla_attention(q_mla, kv_mla, big, batch=batch, seq_len=seq_len,
                               kr_col_block=off["small"] // LANES, tq=_pick(seq_len, 512))
        o_dsa = _dsa_attention(big, batch=batch, seq_len=seq_len, off=off, tq=_pick(seq_len, 256), tk=512)

        merged = _conv_branch(big, conv_w_dw[i], conv_b_dw[i], conv_ln_g[i], conv_ln_b[i],
                              w_o_conv[i].astype(BF16), batch=batch, seq_len=seq_len, off=off,
                              d_model=d_model, tt=_pick(seq_len, 128))
        gate_blk = off["gate"] // d_model
        tn = _pick(d_model, 1024)
        merged = _fma_matmul(o_mla, w_o_mla[i].astype(BF16), mul=big, mul_col_block=gate_blk, add=merged,
                             tm=tm, tn=tn, out_dtype=F32)
        merged = _fma_matmul(o_dsa, w_o_dsa[i].astype(BF16), mul=big, mul_col_block=gate_blk + 1, add=merged,
                             tm=tm, tn=tn, out_dtype=BF16)
        xf = _fma_matmul(merged, w_out[i].astype(BF16), add=xf, tm=tm, tn=tn, out_dtype=F32)
        xf = _mlp(xf, mlp_norm[i], w_up[i].astype(BF16), w_down[i].astype(BF16), final_norm,
                  final_norm=(i == depth - 1), tm=tm, tf=512)
    return xf.reshape(batch, seq_len, d_model)
```

```python
import functools

import jax
import jax.numpy as jnp
import numpy as np
from jax import lax
from jax.experimental import pallas as pl
from jax.experimental.pallas import tpu as pltpu

F32 = jnp.float32
BF16 = jnp.bfloat16

ROPE_THETA = 500000.0
NORM_EPS = 1e-6
N_BRANCH = 3

MLA_HEADS = 16
MLA_Q_LORA = 512
MLA_KV_LORA = 512
MLA_NOPE = 128
MLA_ROPE = 64
MLA_V = 128

DSA_HEADS = 16
DSA_KV_HEADS = 4
DSA_HEAD_DIM = 128
DSA_ROT = DSA_HEAD_DIM // 4
IDX_HEADS = 16
IDX_HEAD_DIM = 64
IDX_ROT = IDX_HEAD_DIM // 4
INDEX_TOPK = 256

CONV_WIDTH = 31

LANES = 128
SUBLANES = 8
COL_TILE = 512
CONV_HALO = 32
MASK_VALUE = -1e30
LOG2_E = 1.4426950408889634
HEAD_V = 128
VT_ROWS = HEAD_V + 16
INT_MIN = -2 ** 31
VMEM_LIMIT = 56 * 1024 * 1024

_NT = (((1,), (1,)), ((), ()))


def _cparams(*sem):
    return pltpu.CompilerParams(dimension_semantics=sem, vmem_limit_bytes=VMEM_LIMIT)


def _in_layout(d_model):
    c = d_model
    off = {"q_a": 0, "q_i": 2048, "c_q": 3072, "c_kv": 3584, "conv_a": 4096, "conv_g": 4096 + c,
           "gate": 4096 + 2 * c}
    off["k_a"] = off["gate"] + N_BRANCH * d_model
    off["v_a"] = off["k_a"] + 512
    off["small"] = off["v_a"] + 512
    off["total"] = off["small"] + 512
    return off


def _rope_lane_tables(seq_len, rot_dim, lane_starts):
    half = rot_dim // 2
    inv = ROPE_THETA ** (-jnp.arange(0, rot_dim, 2, dtype=F32) / rot_dim)
    ang = jnp.arange(seq_len, dtype=F32)[:, None] * inv[None, :]
    cos, sin = jnp.cos(ang), jnp.sin(ang)
    ones = lambda n: jnp.ones((seq_len, n), F32)
    zeros = lambda n: jnp.zeros((seq_len, n), F32)
    c_parts, a_parts, b_parts = [], [], []
    pos = 0
    for s0 in lane_starts:
        gap = s0 - pos
        c_parts += [ones(gap), cos, cos]
        a_parts += [zeros(gap), -sin, zeros(half)]
        b_parts += [zeros(gap), zeros(half), sin]
        pos = s0 + 2 * half
    c_parts.append(ones(LANES - pos))
    a_parts.append(zeros(LANES - pos))
    b_parts.append(zeros(LANES - pos))
    cat = lambda p: jnp.concatenate(p, axis=1)
    return (cat(c_parts), cat(a_parts), cat(b_parts)), half


def _rms(x, gain):
    ms = jnp.mean(x * x, axis=-1, keepdims=True)
    return x * lax.rsqrt(ms + NORM_EPS) * gain


def _proj_kernel(*refs, norm, has_rms, rope_halves, kinds, hilo_scale):
    it = iter(refs)
    a_ref = next(it)
    ga_ref = next(it) if norm else None
    w_ref = next(it)
    rg_ref = next(it) if has_rms else None
    tabs = [(next(it), next(it), next(it)) for _ in rope_halves]
    o_ref = next(it)
    u_ref = next(it) if norm else None
    j = pl.program_id(1)

    if norm:
        @pl.when(j == 0)
        def _():
            u_ref[...] = _rms(a_ref[...], ga_ref[...]).astype(BF16)
        a = u_ref[...]
    else:
        a = a_ref[...]
    acc = jnp.dot(a, w_ref[...], preferred_element_type=F32)

    def chunk_epilogue(op, x):
        if op[0] == "plain":
            return x
        if op[0] == "sigmoid":
            return jax.nn.sigmoid(x)
        if op[0] == "rope":
            c_ref, sa_ref, sb_ref = tabs[op[1]]
            half = rope_halves[op[1]]
            return (x * c_ref[...] + pltpu.roll(x, LANES - half, 1) * sa_ref[...]
                    + pltpu.roll(x, half, 1) * sb_ref[...])
        if op[0] == "hilo":
            v = x * hilo_scale
            lo = v - v.astype(o_ref.dtype).astype(F32)
            lane = lax.broadcasted_iota(jnp.int32, x.shape, 1)
            return jnp.where(lane < IDX_HEADS, v, lo)
        raise ValueError(op)

    def tile_epilogue(kind):
        if kind == "rms":
            o_ref[...] = _rms(acc, rg_ref[...]).astype(o_ref.dtype)
            return
        for c, op in enumerate(kind):
            sl = slice(c * LANES, (c + 1) * LANES)
            o_ref[:, sl] = chunk_epilogue(op, acc[:, sl]).astype(o_ref.dtype)

    if len(kinds) == 1:
        tile_epilogue(kinds[0][0])
    else:
        for kind, ranges in kinds:
            cond = None
            for lo, hi in ranges:
                c = (j >= lo) & (j < hi)
                cond = c if cond is None else (cond | c)
            pl.when(cond)(functools.partial(tile_epilogue, kind))


def _project(a, w, *, a_col_block=0, gain=None, rms_gain=None, rope_tabs=(), rope_halves=(), kinds,
             seq_len, tm, out_dtype=BF16, hilo_scale=1.0):
    n_rows = a.shape[0]
    k_dim, n_cols = w.shape
    tn = COL_TILE
    norm = gain is not None
    has_rms = rms_gain is not None
    t_blocks = seq_len // tm
    in_specs = [pl.BlockSpec((tm, k_dim), lambda i, j: (i, a_col_block))]
    args = [a]
    if norm:
        in_specs.append(pl.BlockSpec((1, k_dim), lambda i, j: (0, 0)))
        args.append(gain.reshape(1, k_dim).astype(F32))
    in_specs.append(pl.BlockSpec((k_dim, tn), lambda i, j: (0, j)))
    args.append(w)
    if has_rms:
        in_specs.append(pl.BlockSpec((1, tn), lambda i, j: (0, j)))
        args.append(rms_gain)
    for tab in rope_tabs:
        for t in tab:
            in_specs.append(pl.BlockSpec((tm, LANES), lambda i, j: (i % t_blocks, 0)))
            args.append(t)
    scratch = [pltpu.VMEM((tm, k_dim), BF16)] if norm else []
    return pl.pallas_call(
        functools.partial(_proj_kernel, norm=norm, has_rms=has_rms, rope_halves=tuple(rope_halves),
                          kinds=kinds, hilo_scale=hilo_scale),
        grid=(n_rows // tm, n_cols // tn),
        in_specs=in_specs,
        out_specs=pl.BlockSpec((tm, tn), lambda i, j: (i, j)),
        out_shape=jax.ShapeDtypeStruct((n_rows, n_cols), out_dtype),
        scratch_shapes=scratch,
        compiler_params=_cparams("parallel", "arbitrary"),
    )(*args)


def _fma_mm_kernel(*refs, has_mul, has_add):
    it = iter(refs)
    a_ref, w_ref = next(it), next(it)
    mul_ref = next(it) if has_mul else None
    add_ref = next(it) if has_add else None
    o_ref = next(it)
    acc = jnp.dot(a_ref[...], w_ref[...], preferred_element_type=F32)
    if has_mul:
        acc = acc * mul_ref[...].astype(F32)
    if has_add:
        acc = acc + add_ref[...].astype(F32)
    o_ref[...] = acc.astype(o_ref.dtype)


def _fma_matmul(a, w, *, mul=None, mul_col_block=0, add=None, tm, tn, out_dtype):
    n_rows, k_dim = a.shape
    n_cols = w.shape[1]
    nj = n_cols // tn
    in_specs = [pl.BlockSpec((tm, k_dim), lambda i, j: (i, 0)),
                pl.BlockSpec((k_dim, tn), lambda i, j: (0, j))]
    args = [a, w]
    if mul is not None:
        in_specs.append(pl.BlockSpec((tm, tn), lambda i, j: (i, mul_col_block * nj + j)))
        args.append(mul)
    if add is not None:
        in_specs.append(pl.BlockSpec((tm, tn), lambda i, j: (i, j)))
        args.append(add)
    return pl.pallas_call(
        functools.partial(_fma_mm_kernel, has_mul=mul is not None, has_add=add is not None),
        grid=(n_rows // tm, nj),
        in_specs=in_specs,
        out_specs=pl.BlockSpec((tm, tn), lambda i, j: (i, j)),
        out_shape=jax.ShapeDtypeStruct((n_rows, n_cols), out_dtype),
        compiler_params=_cparams("parallel", "arbitrary"),
    )(*args)


def _store_transposed_values(vt_ref, idx, v):
    r = lax.broadcasted_iota(jnp.int32, (HEAD_V, HEAD_V), 0)
    c = lax.broadcasted_iota(jnp.int32, (HEAD_V, HEAD_V), 1)
    eye = jnp.where(r == c, 1.0, 0.0).astype(BF16)
    vt_ref[idx, :HEAD_V, :] = lax.dot_general(eye, v, _NT, preferred_element_type=F32).astype(BF16)
    vt_ref[idx, HEAD_V:, :] = jnp.ones((VT_ROWS - HEAD_V, v.shape[0]), BF16)


def _softmax_step(s, m, acc, vt):
    m_new = jnp.maximum(m, jnp.max(s, axis=0, keepdims=True))
    alpha = jnp.exp2(m - m_new)
    p = jnp.exp2(s - m_new).astype(BF16)
    return m_new, alpha * acc + jnp.dot(vt, p, preferred_element_type=F32)


def _softmax_init(tq):
    return jnp.full((1, tq), MASK_VALUE, F32), jnp.zeros((VT_ROWS, tq), F32)


def _softmax_result(acc):
    return jnp.transpose(acc[:HEAD_V] / acc[HEAD_V:HEAD_V + 1])


def _mla_flash_kernel(q_ref, kn_ref, v_ref, kr_ref, o_ref, kcat_ref, vt_ref, *, tq, tk, scale):
    qi = pl.program_id(2)
    per_tile = tq // tk

    @pl.when(qi == 0)
    def _():
        kcat_ref[:, :LANES] = kn_ref[...]
        kcat_ref[:, LANES:] = kr_ref[...]
        for c in range(vt_ref.shape[0]):
            _store_transposed_values(vt_ref, c, v_ref[c * tk:(c + 1) * tk, :])

    q = q_ref[...]

    def block(j, carry, diag=None):
        off = pl.multiple_of(j * tk, tk)
        s = lax.dot_general(kcat_ref[pl.ds(off, tk), :], q, _NT, preferred_element_type=F32) * scale
        if diag is not None:
            key_pos = diag * tk + lax.broadcasted_iota(jnp.int32, s.shape, 0)
            query_pos = lax.broadcasted_iota(jnp.int32, s.shape, 1)
            s = jnp.where(key_pos <= query_pos, s, MASK_VALUE)
        return _softmax_step(s, *carry, vt_ref[j])

    carry = lax.fori_loop(0, qi * per_tile, block, _softmax_init(tq))
    for d in range(per_tile):
        carry = block(qi * per_tile + d, carry, diag=d)
    o_ref[...] = _softmax_result(carry[1]).astype(o_ref.dtype)


def _mla_attention(q, kv, big, *, batch, seq_len, kr_col_block, tq, tk):
    n_rows = batch * seq_len
    nq = seq_len // tq
    scale = float((MLA_NOPE + MLA_ROPE) ** -0.5 * LOG2_E)
    return pl.pallas_call(
        functools.partial(_mla_flash_kernel, tq=tq, tk=tk, scale=scale),
        grid=(batch, MLA_HEADS, nq),
        in_specs=[
            pl.BlockSpec((tq, 2 * LANES), lambda b, h, i: (b * nq + i, h)),
            pl.BlockSpec((seq_len, LANES), lambda b, h, i: (b, h)),
            pl.BlockSpec((seq_len, LANES), lambda b, h, i: (b, MLA_HEADS + h)),
            pl.BlockSpec((seq_len, LANES), lambda b, h, i: (b, kr_col_block)),
        ],
        out_specs=pl.BlockSpec((tq, MLA_V), lambda b, h, i: (b * nq + i, h)),
        out_shape=jax.ShapeDtypeStruct((n_rows, MLA_HEADS * MLA_V), BF16),
        scratch_shapes=[pltpu.VMEM((seq_len, 2 * LANES), BF16), pltpu.VMEM((seq_len // tk, VT_ROWS, tk), BF16)],
        compiler_params=_cparams("arbitrary", "arbitrary", "arbitrary"),
    )(q, kv, kv, big)


def _dsa_kernel(qa_ref, qi_ref, w_ref, ka_ref, va_ref, ke_ref, ko_ref, o_ref,
                key_ref, bias_ref, wt_ref, vt_ref, qs_ref, *, tq, tk, top_k, seq_bits, scale):
    i = pl.program_id(1)
    n_chunks = (i * tq + tq + tk - 1) // tk
    n_groups = tk // SUBLANES
    query_pos = i * tq + lax.broadcasted_iota(jnp.int32, (tk, tq), 1)
    key_in_chunk = lax.broadcasted_iota(jnp.int32, (tk, tq), 0)
    key_in_group = lax.broadcasted_iota(jnp.int32, (SUBLANES, tq), 0)

    @pl.when(i == 0)
    def _():
        for g in range(DSA_KV_HEADS):
            for c in range(key_ref.shape[0]):
                v = va_ref[c * tk:(c + 1) * tk, g * DSA_HEAD_DIM:(g + 1) * DSA_HEAD_DIM]
                _store_transposed_values(vt_ref, g * key_ref.shape[0] + c, v)

    wf = w_ref[...].astype(F32)
    wt_ref[...] = jnp.transpose(wf + pltpu.roll(wf, LANES - IDX_HEADS, 1))

    grp = DSA_HEADS // DSA_KV_HEADS
    for h in range(DSA_HEADS):
        qs_ref[h // grp, (h % grp) * tq:(h % grp + 1) * tq, :] = qa_ref[:, h * DSA_HEAD_DIM:(h + 1) * DSA_HEAD_DIM]

    def score_chunk(c, _):
        off = pl.multiple_of(c * tk, tk)
        ke = ke_ref[pl.ds(off, tk), :]
        ko = ko_ref[pl.ds(off, tk), :]
        sc = jnp.zeros((tk, tq), F32)
        for ch in range(IDX_HEADS // 2):
            qc = qi_ref[:, ch * LANES:(ch + 1) * LANES]
            de = lax.dot_general(ke, qc, _NT, preferred_element_type=F32)
            do = lax.dot_general(ko, qc, _NT, preferred_element_type=F32)
            sc = (sc + wt_ref[2 * ch:2 * ch + 1, :] * jnp.maximum(de, 0.0)
                  + wt_ref[2 * ch + 1:2 * ch + 2, :] * jnp.maximum(do, 0.0))
        sc = jnp.where(sc == 0.0, 0.0, sc)
        bits = pltpu.bitcast(sc, jnp.int32)
        key = jnp.where(bits < 0, bits ^ jnp.int32(0x7FFFFFFF), bits)
        key_ref[c] = jnp.where(off + key_in_chunk <= query_pos, key, jnp.int32(INT_MIN))
        return 0

    lax.fori_loop(0, n_chunks, score_chunk, 0)

    def count(pred):
        def body(c, parts):
            parts = list(parts)
            for r in range(n_groups):
                key = key_ref[c, r * SUBLANES:(r + 1) * SUBLANES, :]
                hit = jnp.where(pred(key, c * tk + r * SUBLANES + key_in_group), 1, 0)
                parts[r % len(parts)] = parts[r % len(parts)] + hit
            return tuple(parts)
        parts = lax.fori_loop(0, n_chunks, body, (jnp.zeros((SUBLANES, tq), jnp.int32),) * 4)
        return jnp.sum(parts[0] + parts[1] + parts[2] + parts[3], axis=0, keepdims=True)

    zero = jnp.zeros((1, tq), jnp.int32)
    thr = jnp.where(count(lambda key, col: key >= zero) >= top_k, zero, jnp.int32(INT_MIN))

    def search(t, thr):
        cand = thr | (jnp.int32(1) << (30 - t))
        return jnp.where(count(lambda key, col: key >= cand) >= top_k, cand, thr)

    thr = lax.fori_loop(0, 31, search, thr)
    thr = jnp.maximum(thr, jnp.int32(INT_MIN + 1))

    need = top_k - count(lambda key, col: key > thr)
    n_eq = count(lambda key, col: key == thr)
    has_excess = jnp.max(jnp.where(n_eq > need, 1, 0)) > 0

    def tie_cut():
        def step(t, cut):
            cand = cut | (jnp.int32(1) << (seq_bits - 1 - t))
            cnt = count(lambda key, col: (key == thr) & (col < cand))
            return jnp.where(cnt < need, cand, cut)
        return lax.fori_loop(0, seq_bits, step, zero)

    cut = lax.cond(has_excess, tie_cut, lambda: jnp.full((1, tq), 2 ** seq_bits, jnp.int32))

    def bias_chunk(c, _):
        key = key_ref[c]
        sel = (key > thr) | ((key == thr) & (c * tk + key_in_chunk <= cut))
        bias_ref[c] = jnp.where(sel, 0.0, MASK_VALUE)
        return 0

    lax.fori_loop(0, n_chunks, bias_chunk, 0)

    for g in range(DSA_KV_HEADS):
        gs = slice(g * DSA_HEAD_DIM, (g + 1) * DSA_HEAD_DIM)

        def attend(c, carry, gs=gs, g=g):
            m, acc = carry
            off = pl.multiple_of(c * tk, tk)
            s = lax.dot_general(ka_ref[pl.ds(off, tk), gs], qs_ref[g], _NT, preferred_element_type=F32)
            bias = bias_ref[c]
            s = s * scale + jnp.concatenate([bias] * grp, axis=1)
            return _softmax_step(s, m, acc, vt_ref[g * key_ref.shape[0] + c])

        _, acc = lax.fori_loop(0, n_chunks, attend, _softmax_init(grp * tq))
        out = acc[:HEAD_V] / acc[HEAD_V:HEAD_V + 1]
        for r in range(grp):
            h = g * grp + r
            o_ref[:, h * DSA_HEAD_DIM:(h + 1) * DSA_HEAD_DIM] = jnp.transpose(
                out[:, r * tq:(r + 1) * tq]).astype(o_ref.dtype)


def _dsa_attention(big, *, batch, seq_len, off, tq, tk):
    n_rows = batch * seq_len
    nq = seq_len // tq
    top_k = min(INDEX_TOPK, seq_len // 4)
    seq_bits = int(seq_len).bit_length() - 1
    assert 2 ** seq_bits == seq_len and tk >= top_k
    qa_w, qi_w, kv_w = DSA_HEADS * DSA_HEAD_DIM, IDX_HEADS * IDX_HEAD_DIM, DSA_KV_HEADS * DSA_HEAD_DIM
    small = off["small"] // LANES
    rowmap = lambda col: (lambda b, i: (b * nq + i, col))
    seqmap = lambda col: (lambda b, i: (b, col))
    n_ck = seq_len // tk
    return pl.pallas_call(
        functools.partial(_dsa_kernel, tq=tq, tk=tk, top_k=top_k, seq_bits=seq_bits,
                          scale=float(DSA_HEAD_DIM ** -0.5 * LOG2_E)),
        grid=(batch, nq),
        in_specs=[
            pl.BlockSpec((tq, qa_w), rowmap(off["q_a"] // qa_w)),
            pl.BlockSpec((tq, qi_w), rowmap(off["q_i"] // qi_w)),
            pl.BlockSpec((tq, LANES), rowmap(small + 3)),
            pl.BlockSpec((seq_len, kv_w), seqmap(off["k_a"] // kv_w)),
            pl.BlockSpec((seq_len, kv_w), seqmap(off["v_a"] // kv_w)),
            pl.BlockSpec((seq_len, LANES), seqmap(small + 1)),
            pl.BlockSpec((seq_len, LANES), seqmap(small + 2)),
        ],
        out_specs=pl.BlockSpec((tq, qa_w), lambda b, i: (b * nq + i, 0)),
        out_shape=jax.ShapeDtypeStruct((n_rows, qa_w), BF16),
        scratch_shapes=[pltpu.VMEM((n_ck, tk, tq), jnp.int32), pltpu.VMEM((n_ck, tk, tq), F32),
                        pltpu.VMEM((LANES, tq), F32), pltpu.VMEM((DSA_KV_HEADS * n_ck, VT_ROWS, tk), BF16),
                        pltpu.VMEM((DSA_KV_HEADS, DSA_HEADS // DSA_KV_HEADS * tq, DSA_HEAD_DIM), BF16)],
        compiler_params=_cparams("arbitrary", "arbitrary"),
    )(big, big, big, big, big, big, big)


def _conv_kernel(a_ref, g_ref, ha_ref, hg_ref, gate_ref, wdw_ref, bdw_ref, lng_ref, lnb_ref, wo_ref,
                 o_ref, y_ref, c_ref, *, tt, rows, cols):
    t = pl.program_id(1)
    y_ref[CONV_HALO:, :] = a_ref[...].astype(F32) * jax.nn.sigmoid(g_ref[...].astype(F32))
    halo = ha_ref[...].astype(F32) * jax.nn.sigmoid(hg_ref[...].astype(F32))
    y_ref[:CONV_HALO, :] = jnp.where(t == 0, 0.0, halo)

    n_ch = y_ref.shape[1]
    lead = CONV_HALO - (CONV_WIDTH - 1)
    for c0 in range(0, n_ch, cols):
        cs = slice(c0, c0 + cols)
        for r0 in range(0, tt, rows):
            acc = jnp.broadcast_to(bdw_ref[:, cs], (rows, cols))
            for j in range(CONV_WIDTH):
                acc = acc + wdw_ref[j:j + 1, cs] * y_ref[r0 + lead + j:r0 + lead + j + rows, cs]
            c_ref[r0:r0 + rows, cs] = acc

    x = c_ref[...]
    mu = jnp.mean(x, axis=-1, keepdims=True)
    xc = x - mu
    var = jnp.mean(xc * xc, axis=-1, keepdims=True)
    yn = xc * lax.rsqrt(var + NORM_EPS) * lng_ref[...] + lnb_ref[...]
    z = yn * jax.nn.sigmoid(yn)
    proj = jnp.dot(z.astype(BF16), wo_ref[...], preferred_element_type=F32)
    o_ref[...] = (gate_ref[...].astype(F32) * proj).astype(o_ref.dtype)


def _conv_branch(big, w_dw, b_dw, ln_g, ln_b, w_o, *, batch, seq_len, off, d_model, tt):
    n_rows = batch * seq_len
    n_ch = w_dw.shape[1]
    nt = seq_len // tt
    hb = tt // CONV_HALO
    w_pad = jnp.concatenate([w_dw.astype(F32), jnp.zeros((CONV_HALO - CONV_WIDTH, n_ch), F32)], axis=0)
    ca, cg, gc = off["conv_a"] // n_ch, off["conv_g"] // n_ch, off["gate"] // d_model + 2
    cur = lambda col: (lambda b, t: (b * nt + t, col))
    halo = lambda col: (lambda b, t: (jnp.maximum((b * nt + t) * hb - 1, 0), col))
    const = lambda b, t: (0, 0)
    vec = lambda v: v.reshape(1, n_ch).astype(F32)
    return pl.pallas_call(
        functools.partial(_conv_kernel, tt=tt, rows=32, cols=512),
        grid=(batch, nt),
        in_specs=[
            pl.BlockSpec((tt, n_ch), cur(ca)),
            pl.BlockSpec((tt, n_ch), cur(cg)),
            pl.BlockSpec((CONV_HALO, n_ch), halo(ca)),
            pl.BlockSpec((CONV_HALO, n_ch), halo(cg)),
            pl.BlockSpec((tt, d_model), cur(gc)),
            pl.BlockSpec((CONV_HALO, n_ch), const),
            pl.BlockSpec((1, n_ch), const),
            pl.BlockSpec((1, n_ch), const),
            pl.BlockSpec((1, n_ch), const),
            pl.BlockSpec((n_ch, d_model), const),
        ],
        out_specs=pl.BlockSpec((tt, d_model), lambda b, t: (b * nt + t, 0)),
        out_shape=jax.ShapeDtypeStruct((n_rows, d_model), F32),
        scratch_shapes=[pltpu.VMEM((tt + CONV_HALO, n_ch), F32), pltpu.VMEM((tt, n_ch), F32)],
        compiler_params=_cparams("parallel", "arbitrary"),
    )(big, big, big, big, big, w_pad, vec(b_dw), vec(ln_g), vec(ln_b), w_o)


def _mlp_kernel(x_ref, g_ref, wu_ref, wd_ref, gf_ref, o_ref, u_ref, acc_ref, *, final_norm):
    f = pl.program_id(1)

    @pl.when(f == 0)
    def _():
        u_ref[...] = _rms(x_ref[...], g_ref[...]).astype(BF16)
        acc_ref[...] = jnp.zeros_like(acc_ref)

    h = jnp.maximum(jnp.dot(u_ref[...], wu_ref[...], preferred_element_type=F32), 0.0)
    acc_ref[...] += jnp.dot((h * h).astype(BF16), wd_ref[...], preferred_element_type=F32)

    @pl.when(f == pl.num_programs(1) - 1)
    def _():
        y = x_ref[...] + acc_ref[...]
        if final_norm:
            y = _rms(y, gf_ref[...])
        o_ref[...] = y


def _mlp(x, gain, w_up, w_down, final_gain, *, final_norm, tm, tf):
    n_rows, d_model = x.shape
    d_ff = w_up.shape[1]
    vec = lambda v: v.reshape(1, d_model).astype(F32)
    return pl.pallas_call(
        functools.partial(_mlp_kernel, final_norm=final_norm),
        grid=(n_rows // tm, d_ff // tf),
        in_specs=[
            pl.BlockSpec((tm, d_model), lambda i, f: (i, 0)),
            pl.BlockSpec((1, d_model), lambda i, f: (0, 0)),
            pl.BlockSpec((d_model, tf), lambda i, f: (0, f)),
            pl.BlockSpec((tf, d_model), lambda i, f: (f, 0)),
            pl.BlockSpec((1, d_model), lambda i, f: (0, 0)),
        ],
        out_specs=pl.BlockSpec((tm, d_model), lambda i, f: (i, 0)),
        out_shape=jax.ShapeDtypeStruct((n_rows, d_model), F32),
        scratch_shapes=[pltpu.VMEM((tm, d_model), BF16), pltpu.VMEM((tm, d_model), F32)],
        compiler_params=_cparams("parallel", "arbitrary"),
    )(x, vec(gain), w_up, w_down, vec(final_gain))


def _prep_w_in(w_in, d_model):
    widths = [MLA_Q_LORA, MLA_KV_LORA, MLA_ROPE, DSA_HEADS * DSA_HEAD_DIM, DSA_KV_HEADS * DSA_HEAD_DIM,
              DSA_KV_HEADS * DSA_HEAD_DIM, IDX_HEADS * IDX_HEAD_DIM, IDX_HEAD_DIM, IDX_HEADS,
              2 * d_model, N_BRANCH * d_model]
    splits = [int(v) for v in np.cumsum(widths)[:-1]]
    c_q, c_kv, k_rope, q_a, k_a, v_a, q_i, k_i, w_i, conv_in, gates = jnp.split(w_in, splits, axis=1)
    z = lambda n: jnp.zeros((w_in.shape[0], n), w_in.dtype)
    small = jnp.concatenate([k_rope, z(64), k_i, z(64), z(64), k_i, w_i, w_i, z(LANES - 2 * IDX_HEADS)], axis=1)
    return jnp.concatenate([q_a, q_i, c_q, c_kv, conv_in, gates, k_a, v_a, small], axis=1).astype(BF16)


def _prep_w_uq(w_uq):
    k = w_uq.shape[0]
    w = w_uq.reshape(k, MLA_HEADS, MLA_NOPE + MLA_ROPE)
    w = jnp.concatenate([w, jnp.zeros((k, MLA_HEADS, 2 * LANES - MLA_NOPE - MLA_ROPE), w.dtype)], axis=2)
    return w.reshape(k, MLA_HEADS * 2 * LANES).astype(BF16)


def _prep_w_ukv(w_ukv):
    k = w_ukv.shape[0]
    w = w_ukv.reshape(k, MLA_HEADS, 2, MLA_NOPE)
    return jnp.transpose(w, (0, 2, 1, 3)).reshape(k, 2 * MLA_HEADS * MLA_NOPE).astype(BF16)


def _tile_ranges(off, d_model):
    t = lambda name: off[name] // COL_TILE
    rope_dsa = (("rope", 1),) * 4
    rope_idx = (("rope", 2),) * 4
    plain = (("plain",),) * 4
    sigm = (("sigmoid",),) * 4
    small = (("rope", 0), ("rope", 2), ("rope", 2), ("hilo",))
    return (
        (rope_dsa, ((t("q_a"), t("q_i")), (t("k_a"), t("v_a")))),
        (rope_idx, ((t("q_i"), t("c_q")),)),
        ("rms", ((t("c_q"), t("conv_a")),)),
        (plain, ((t("conv_a"), t("gate")), (t("v_a"), t("small")))),
        (sigm, ((t("gate"), t("k_a")),)),
        (small, ((t("small"), t("total")),)),
    )


def _pick(n, pref):
    return pref if n % pref == 0 else n


def kernel(x, attn_norm, w_in, mla_q_norm, mla_kv_norm, mla_w_uq, mla_w_ukv, conv_w_dw, conv_b_dw,
           conv_ln_g, conv_ln_b, w_o_mla, w_o_dsa, w_o_conv, w_out, mlp_norm, w_up, w_down, final_norm):
    batch, seq_len, d_model = x.shape
    depth = w_in.shape[0]
    n_rows = batch * seq_len
    off = _in_layout(d_model)
    kinds = _tile_ranges(off, d_model)

    tab_mla, half_mla = _rope_lane_tables(seq_len, MLA_ROPE, (0,))
    tab_dsa, half_dsa = _rope_lane_tables(seq_len, DSA_ROT, (0,))
    tab_idx, half_idx = _rope_lane_tables(seq_len, IDX_ROT, (0, IDX_HEAD_DIM))
    tabs = (tab_mla, tab_dsa, tab_idx)
    halves = (half_mla, half_dsa, half_idx)

    tm = _pick(seq_len, 512)
    xf = x.reshape(n_rows, d_model)
    for i in range(depth):
        ones = jnp.ones((off["total"],), F32)
        rms_gain = lax.dynamic_update_slice(ones, mla_q_norm[i].astype(F32), (off["c_q"],))
        rms_gain = lax.dynamic_update_slice(rms_gain, mla_kv_norm[i].astype(F32), (off["c_kv"],))
        big = _project(xf, _prep_w_in(w_in[i], d_model), gain=attn_norm[i], rms_gain=rms_gain.reshape(1, -1),
                       rope_tabs=tabs, rope_halves=halves, kinds=kinds, seq_len=seq_len, tm=tm,
                       hilo_scale=float((IDX_HEADS * IDX_HEAD_DIM) ** -0.5))

        q_mla = _project(big, _prep_w_uq(mla_w_uq[i]), a_col_block=off["c_q"] // MLA_Q_LORA,
                         rope_tabs=(tab_mla,), rope_halves=(half_mla,),
                         kinds=(((("plain",), ("rope", 0)) * 2, ((0, 1 << 30),)),), seq_len=seq_len, tm=tm)
        kv_mla = _project(big, _prep_w_ukv(mla_w_ukv[i]), a_col_block=off["c_kv"] // MLA_KV_LORA,
                          kinds=(((("plain",),) * 4, ((0, 1 << 30),)),), seq_len=seq_len, tm=tm)
        o_mla = _mla_attention(q_mla, kv_mla, big, batch=batch, seq_len=seq_len,
                               kr_col_block=off["small"] // LANES, tq=_pick(seq_len, 1024), tk=512)
        o_dsa = _dsa_attention(big, batch=batch, seq_len=seq_len, off=off, tq=_pick(seq_len, 256), tk=512)

        merged = _conv_branch(big, conv_w_dw[i], conv_b_dw[i], conv_ln_g[i], conv_ln_b[i],
                              w_o_conv[i].astype(BF16), batch=batch, seq_len=seq_len, off=off,
                              d_model=d_model, tt=_pick(seq_len, 128))
        gate_blk = off["gate"] // d_model
        tn = _pick(d_model, 1024)
        merged = _fma_matmul(o_mla, w_o_mla[i].astype(BF16), mul=big, mul_col_block=gate_blk, add=merged,
                             tm=tm, tn=tn, out_dtype=F32)
        merged = _fma_matmul(o_dsa, w_o_dsa[i].astype(BF16), mul=big, mul_col_block=gate_blk + 1, add=merged,
                             tm=tm, tn=tn, out_dtype=BF16)
        xf = _fma_matmul(merged, w_out[i].astype(BF16), add=xf, tm=tm, tn=tn, out_dtype=F32)
        xf = _mlp(xf, mlp_norm[i], w_up[i].astype(BF16), w_down[i].astype(BF16), final_norm,
                  final_norm=(i == depth - 1), tm=tm, tf=512)
    return xf.reshape(batch, seq_len, d_model)
```

```python
import functools

import jax
import jax.numpy as jnp
import numpy as np
from jax import lax
from jax.experimental import pallas as pl
from jax.experimental.pallas import tpu as pltpu

F32 = jnp.float32
BF16 = jnp.bfloat16

ROPE_THETA = 500000.0
NORM_EPS = 1e-6
N_BRANCH = 3

MLA_HEADS = 16
MLA_Q_LORA = 512
MLA_KV_LORA = 512
MLA_NOPE = 128
MLA_ROPE = 64
MLA_V = 128

DSA_HEADS = 16
DSA_KV_HEADS = 4
DSA_HEAD_DIM = 128
DSA_ROT = DSA_HEAD_DIM // 4
IDX_HEADS = 16
IDX_HEAD_DIM = 64
IDX_ROT = IDX_HEAD_DIM // 4
INDEX_TOPK = 256

CONV_WIDTH = 31

LANES = 128
SUBLANES = 8
COL_TILE = 512
CONV_HALO = 32
CONV_ROWS, CONV_COLS = 32, 512
MASK_VALUE = -1e30
LOG2_E = 1.4426950408889634
HEAD_V = 128
VT_ROWS = HEAD_V + 16
INT_MIN = -2 ** 31
VMEM_LIMIT = 56 * 1024 * 1024

_NT = (((1,), (1,)), ((), ()))


def _cparams(*sem):
    return pltpu.CompilerParams(dimension_semantics=sem, vmem_limit_bytes=VMEM_LIMIT)


def _in_layout(d_model):
    c = d_model
    off = {"q_a": 0, "q_i": 2048, "c_q": 3072, "c_kv": 3584, "conv_a": 4096, "conv_g": 4096 + c,
           "gate": 4096 + 2 * c}
    off["k_a"] = off["gate"] + N_BRANCH * d_model
    off["v_a"] = off["k_a"] + 512
    off["small"] = off["v_a"] + 512
    off["total"] = off["small"] + 512
    return off


def _rope_lane_tables(seq_len, rot_dim, lane_starts):
    half = rot_dim // 2
    inv = ROPE_THETA ** (-jnp.arange(0, rot_dim, 2, dtype=F32) / rot_dim)
    ang = jnp.arange(seq_len, dtype=F32)[:, None] * inv[None, :]
    cos, sin = jnp.cos(ang), jnp.sin(ang)
    ones = lambda n: jnp.ones((seq_len, n), F32)
    zeros = lambda n: jnp.zeros((seq_len, n), F32)
    c_parts, a_parts, b_parts = [], [], []
    pos = 0
    for s0 in lane_starts:
        gap = s0 - pos
        c_parts += [ones(gap), cos, cos]
        a_parts += [zeros(gap), -sin, zeros(half)]
        b_parts += [zeros(gap), zeros(half), sin]
        pos = s0 + 2 * half
    c_parts.append(ones(LANES - pos))
    a_parts.append(zeros(LANES - pos))
    b_parts.append(zeros(LANES - pos))
    cat = lambda p: jnp.concatenate(p, axis=1)
    return (cat(c_parts), cat(a_parts), cat(b_parts)), half


def _rms(x, gain):
    ms = jnp.mean(x * x, axis=-1, keepdims=True)
    return x * lax.rsqrt(ms + NORM_EPS) * gain


def _proj_kernel(*refs, norm, has_rms, rope_halves, kinds):
    it = iter(refs)
    a_ref = next(it)
    ga_ref = next(it) if norm else None
    w_ref = next(it)
    rg_ref = next(it) if has_rms else None
    tabs = [(next(it), next(it), next(it)) for _ in rope_halves]
    o_ref = next(it)
    u_ref = next(it) if norm else None
    j = pl.program_id(1)

    if norm:
        @pl.when(j == 0)
        def _():
            u_ref[...] = _rms(a_ref[...], ga_ref[...]).astype(BF16)

    def chunk_epilogue(op, x):
        name, arg, scale = op
        if name == "plain":
            y = x
        elif name == "sigmoid":
            y = jax.nn.sigmoid(x)
        elif name == "rope":
            c_ref, sa_ref, sb_ref = tabs[arg]
            half = rope_halves[arg]
            y = (x * c_ref[...] + pltpu.roll(x, LANES - half, 1) * sa_ref[...]
                 + pltpu.roll(x, half, 1) * sb_ref[...])
        elif name == "hilo":
            v = x * scale
            lo = v - v.astype(o_ref.dtype).astype(F32)
            lane = lax.broadcasted_iota(jnp.int32, x.shape, 1)
            return jnp.where(lane < IDX_HEADS, v, lo)
        else:
            raise ValueError(op)
        return y if scale == 1.0 else y * scale

    def tile(kind):
        a = u_ref[...] if norm else a_ref[...]
        acc = jnp.dot(a, w_ref[...], preferred_element_type=F32)
        if kind == "rms":
            o_ref[...] = _rms(acc, rg_ref[...]).astype(o_ref.dtype)
            return
        for c, op in enumerate(kind):
            sl = slice(c * LANES, (c + 1) * LANES)
            o_ref[:, sl] = chunk_epilogue(op, acc[:, sl]).astype(o_ref.dtype)

    if len(kinds) == 1:
        tile(kinds[0][0])
    else:
        for kind, ranges in kinds:
            cond = None
            for lo, hi in ranges:
                c = (j >= lo) & (j < hi)
                cond = c if cond is None else (cond | c)
            pl.when(cond)(functools.partial(tile, kind))


def _project(a, w, *, a_col_block=0, gain=None, rms_gain=None, rope_tabs=(), rope_halves=(), kinds,
             seq_len, tm, tn=COL_TILE, out_dtype=BF16):
    n_rows = a.shape[0]
    k_dim, n_cols = w.shape
    norm = gain is not None
    has_rms = rms_gain is not None
    t_blocks = seq_len // tm
    in_specs = [pl.BlockSpec((tm, k_dim), lambda i, j: (i, a_col_block))]
    args = [a]
    if norm:
        in_specs.append(pl.BlockSpec((1, k_dim), lambda i, j: (0, 0)))
        args.append(gain.reshape(1, k_dim).astype(F32))
    in_specs.append(pl.BlockSpec((k_dim, tn), lambda i, j: (0, j)))
    args.append(w)
    if has_rms:
        in_specs.append(pl.BlockSpec((1, tn), lambda i, j: (0, j)))
        args.append(rms_gain)
    for tab in rope_tabs:
        for t in tab:
            in_specs.append(pl.BlockSpec((tm, LANES), lambda i, j: (i % t_blocks, 0)))
            args.append(t)
    scratch = [pltpu.VMEM((tm, k_dim), BF16)] if norm else []
    return pl.pallas_call(
        functools.partial(_proj_kernel, norm=norm, has_rms=has_rms, rope_halves=tuple(rope_halves),
                          kinds=kinds),
        grid=(n_rows // tm, n_cols // tn),
        in_specs=in_specs,
        out_specs=pl.BlockSpec((tm, tn), lambda i, j: (i, j)),
        out_shape=jax.ShapeDtypeStruct((n_rows, n_cols), out_dtype),
        scratch_shapes=scratch,
        compiler_params=_cparams("parallel", "arbitrary"),
    )(*args)


def _merge_kernel(oa_ref, ob_ref, ga_ref, gb_ref, mc_ref, x_ref, wa_ref, wb_ref, wo_ref, o_ref):
    ya = jnp.dot(oa_ref[...], wa_ref[...], preferred_element_type=F32)
    merged = mc_ref[...] + ga_ref[...].astype(F32) * ya
    yb = jnp.dot(ob_ref[...], wb_ref[...], preferred_element_type=F32)
    merged = merged + gb_ref[...].astype(F32) * yb
    o_ref[...] = x_ref[...] + jnp.dot(merged.astype(BF16), wo_ref[...], preferred_element_type=F32)


def _merge_project(o_mla, o_dsa, big, gated_conv, x, w_a, w_b, w_out, *, gate_col_block, tm):
    n_rows, d_model = x.shape
    row = lambda col: (lambda i: (i, col))
    resident = lambda w: pl.BlockSpec(w.shape, lambda i: (0, 0), pipeline_mode=pl.Buffered(1))
    return pl.pallas_call(
        _merge_kernel,
        grid=(n_rows // tm,),
        in_specs=[
            pl.BlockSpec((tm, o_mla.shape[1]), row(0)),
            pl.BlockSpec((tm, o_dsa.shape[1]), row(0)),
            pl.BlockSpec((tm, d_model), row(gate_col_block)),
            pl.BlockSpec((tm, d_model), row(gate_col_block + 1)),
            pl.BlockSpec((tm, d_model), row(0)),
            pl.BlockSpec((tm, d_model), row(0)),
            resident(w_a), resident(w_b), resident(w_out),
        ],
        out_specs=pl.BlockSpec((tm, d_model), row(0)),
        out_shape=jax.ShapeDtypeStruct((n_rows, d_model), F32),
        compiler_params=_cparams("parallel"),
    )(o_mla, o_dsa, big, big, gated_conv, x, w_a, w_b, w_out)


def _store_transposed_values(vt_ref, idx, v):
    r = lax.broadcasted_iota(jnp.int32, (HEAD_V, HEAD_V), 0)
    c = lax.broadcasted_iota(jnp.int32, (HEAD_V, HEAD_V), 1)
    eye = jnp.where(r == c, 1.0, 0.0).astype(BF16)
    vt_ref[idx, :HEAD_V, :] = lax.dot_general(eye, v, _NT, preferred_element_type=F32).astype(BF16)
    vt_ref[idx, HEAD_V:, :] = jnp.ones((VT_ROWS - HEAD_V, v.shape[0]), BF16)


def _softmax_step(s, m, acc, vt):
    m_new = jnp.maximum(m, jnp.max(s, axis=0, keepdims=True))
    alpha = jnp.exp2(m - m_new)
    p = jnp.exp2(s - m_new).astype(BF16)
    return m_new, alpha * acc + jnp.dot(vt, p, preferred_element_type=F32)


def _softmax_init(tq):
    return jnp.full((1, tq), MASK_VALUE, F32), jnp.zeros((VT_ROWS, tq), F32)


def _softmax_result(acc):
    return jnp.transpose(acc[:HEAD_V] / acc[HEAD_V:HEAD_V + 1])


def _mla_flash_kernel(q_ref, kn_ref, v_ref, kr_ref, o_ref, kcat_ref, vt_ref, *, tq, tk):
    qi = pl.program_id(2)
    per_tile = tq // tk

    @pl.when(qi == 0)
    def _():
        kcat_ref[:, :LANES] = kn_ref[...]
        kcat_ref[:, LANES:] = kr_ref[...]
        for c in range(vt_ref.shape[0]):
            _store_transposed_values(vt_ref, c, v_ref[c * tk:(c + 1) * tk, :])

    q = q_ref[...]

    def block(j, carry, diag=None):
        off = pl.multiple_of(j * tk, tk)
        s = lax.dot_general(kcat_ref[pl.ds(off, tk), :], q, _NT, preferred_element_type=F32)
        if diag is not None:
            key_pos = diag * tk + lax.broadcasted_iota(jnp.int32, s.shape, 0)
            query_pos = lax.broadcasted_iota(jnp.int32, s.shape, 1)
            s = jnp.where(key_pos <= query_pos, s, MASK_VALUE)
        return _softmax_step(s, *carry, vt_ref[j])

    carry = lax.fori_loop(0, qi * per_tile, block, _softmax_init(tq))
    for d in range(per_tile):
        carry = block(qi * per_tile + d, carry, diag=d)
    o_ref[...] = _softmax_result(carry[1]).astype(o_ref.dtype)


def _mla_attention(q, kv, big, *, batch, seq_len, kr_col_block, tq, tk):
    n_rows = batch * seq_len
    nq = seq_len // tq
    return pl.pallas_call(
        functools.partial(_mla_flash_kernel, tq=tq, tk=tk),
        grid=(batch, MLA_HEADS, nq),
        in_specs=[
            pl.BlockSpec((tq, 2 * LANES), lambda b, h, i: (b * nq + i, h)),
            pl.BlockSpec((seq_len, LANES), lambda b, h, i: (b, h)),
            pl.BlockSpec((seq_len, LANES), lambda b, h, i: (b, MLA_HEADS + h)),
            pl.BlockSpec((seq_len, LANES), lambda b, h, i: (b, kr_col_block)),
        ],
        out_specs=pl.BlockSpec((tq, MLA_V), lambda b, h, i: (b * nq + i, h)),
        out_shape=jax.ShapeDtypeStruct((n_rows, MLA_HEADS * MLA_V), BF16),
        scratch_shapes=[pltpu.VMEM((seq_len, 2 * LANES), BF16), pltpu.VMEM((seq_len // tk, VT_ROWS, tk), BF16)],
        compiler_params=_cparams("arbitrary", "arbitrary", "arbitrary"),
    )(q, kv, kv, big)


def _dsa_kernel(qa_ref, qi_ref, w_ref, ka_ref, va_ref, ke_ref, ko_ref, o_ref,
                key_ref, bias_ref, wt_ref, vt_ref, qs_ref, *, tq, tk, top_k, seq_bits):
    i = pl.program_id(1)
    n_chunks = (i * tq + tq + tk - 1) // tk
    n_groups = tk // SUBLANES
    query_pos = i * tq + lax.broadcasted_iota(jnp.int32, (tk, tq), 1)
    key_in_chunk = lax.broadcasted_iota(jnp.int32, (tk, tq), 0)
    key_in_group = lax.broadcasted_iota(jnp.int32, (SUBLANES, tq), 0)

    @pl.when(i == 0)
    def _():
        for g in range(DSA_KV_HEADS):
            for c in range(key_ref.shape[0]):
                v = va_ref[c * tk:(c + 1) * tk, g * DSA_HEAD_DIM:(g + 1) * DSA_HEAD_DIM]
                _store_transposed_values(vt_ref, g * key_ref.shape[0] + c, v)

    wf = w_ref[...].astype(F32)
    wt_ref[...] = jnp.transpose(wf + pltpu.roll(wf, LANES - IDX_HEADS, 1))

    grp = DSA_HEADS // DSA_KV_HEADS
    for h in range(DSA_HEADS):
        qs_ref[h // grp, (h % grp) * tq:(h % grp + 1) * tq, :] = qa_ref[:, h * DSA_HEAD_DIM:(h + 1) * DSA_HEAD_DIM]

    def score_chunk(c, _):
        off = pl.multiple_of(c * tk, tk)
        ke = ke_ref[pl.ds(off, tk), :]
        ko = ko_ref[pl.ds(off, tk), :]
        sc = jnp.zeros((tk, tq), F32)
        for ch in range(IDX_HEADS // 2):
            qc = qi_ref[:, ch * LANES:(ch + 1) * LANES]
            de = lax.dot_general(ke, qc, _NT, preferred_element_type=F32)
            do = lax.dot_general(ko, qc, _NT, preferred_element_type=F32)
            sc = (sc + wt_ref[2 * ch:2 * ch + 1, :] * jnp.maximum(de, 0.0)
                  + wt_ref[2 * ch + 1:2 * ch + 2, :] * jnp.maximum(do, 0.0))
        sc = jnp.where(sc == 0.0, 0.0, sc)
        bits = pltpu.bitcast(sc, jnp.int32)
        key = jnp.where(bits < 0, bits ^ jnp.int32(0x7FFFFFFF), bits)
        key_ref[c] = jnp.where(off + key_in_chunk <= query_pos, key, jnp.int32(INT_MIN))
        return 0

    lax.fori_loop(0, n_chunks, score_chunk, 0)

    def count(pred):
        def body(c, parts):
            parts = list(parts)
            for r in range(n_groups):
                key = key_ref[c, r * SUBLANES:(r + 1) * SUBLANES, :]
                hit = jnp.where(pred(key, c * tk + r * SUBLANES + key_in_group), 1, 0)
                parts[r % len(parts)] = parts[r % len(parts)] + hit
            return tuple(parts)
        parts = lax.fori_loop(0, n_chunks, body, (jnp.zeros((SUBLANES, tq), jnp.int32),) * 4)
        return jnp.sum(parts[0] + parts[1] + parts[2] + parts[3], axis=0, keepdims=True)

    zero = jnp.zeros((1, tq), jnp.int32)
    thr = jnp.where(count(lambda key, col: key >= zero) >= top_k, zero, jnp.int32(INT_MIN))

    def search(t, thr):
        cand = thr | (jnp.int32(1) << (30 - t))
        return jnp.where(count(lambda key, col: key >= cand) >= top_k, cand, thr)

    thr = lax.fori_loop(0, 31, search, thr)
    thr = jnp.maximum(thr, jnp.int32(INT_MIN + 1))

    need = top_k - count(lambda key, col: key > thr)
    n_eq = count(lambda key, col: key == thr)
    has_excess = jnp.max(jnp.where(n_eq > need, 1, 0)) > 0

    def tie_cut():
        def step(t, cut):
            cand = cut | (jnp.int32(1) << (seq_bits - 1 - t))
            cnt = count(lambda key, col: (key == thr) & (col < cand))
            return jnp.where(cnt < need, cand, cut)
        return lax.fori_loop(0, seq_bits, step, zero)

    cut = lax.cond(has_excess, tie_cut, lambda: jnp.full((1, tq), 2 ** seq_bits, jnp.int32))

    def bias_chunk(c, _):
        key = key_ref[c]
        sel = (key > thr) | ((key == thr) & (c * tk + key_in_chunk <= cut))
        bias_ref[c] = jnp.where(sel, 0.0, MASK_VALUE)
        return 0

    lax.fori_loop(0, n_chunks, bias_chunk, 0)

    for g in range(DSA_KV_HEADS):
        gs = slice(g * DSA_HEAD_DIM, (g + 1) * DSA_HEAD_DIM)

        def attend(c, carry, gs=gs, g=g):
            m, acc = carry
            off = pl.multiple_of(c * tk, tk)
            s = lax.dot_general(ka_ref[pl.ds(off, tk), gs], qs_ref[g], _NT, preferred_element_type=F32)
            bias = bias_ref[c]
            s = s + jnp.concatenate([bias] * grp, axis=1)
            return _softmax_step(s, m, acc, vt_ref[g * key_ref.shape[0] + c])

        _, acc = lax.fori_loop(0, n_chunks, attend, _softmax_init(grp * tq))
        out = acc[:HEAD_V] / acc[HEAD_V:HEAD_V + 1]
        for r in range(grp):
            h = g * grp + r
            o_ref[:, h * DSA_HEAD_DIM:(h + 1) * DSA_HEAD_DIM] = jnp.transpose(
                out[:, r * tq:(r + 1) * tq]).astype(o_ref.dtype)


def _dsa_attention(big, *, batch, seq_len, off, tq, tk):
    n_rows = batch * seq_len
    nq = seq_len // tq
    top_k = min(INDEX_TOPK, seq_len // 4)
    seq_bits = int(seq_len).bit_length() - 1
    assert 2 ** seq_bits == seq_len and tk >= top_k
    qa_w, qi_w, kv_w = DSA_HEADS * DSA_HEAD_DIM, IDX_HEADS * IDX_HEAD_DIM, DSA_KV_HEADS * DSA_HEAD_DIM
    small = off["small"] // LANES
    rowmap = lambda col: (lambda b, i: (b * nq + i, col))
    seqmap = lambda col: (lambda b, i: (b, col))
    n_ck = seq_len // tk
    return pl.pallas_call(
        functools.partial(_dsa_kernel, tq=tq, tk=tk, top_k=top_k, seq_bits=seq_bits),
        grid=(batch, nq),
        in_specs=[
            pl.BlockSpec((tq, qa_w), rowmap(off["q_a"] // qa_w)),
            pl.BlockSpec((tq, qi_w), rowmap(off["q_i"] // qi_w)),
            pl.BlockSpec((tq, LANES), rowmap(small + 3)),
            pl.BlockSpec((seq_len, kv_w), seqmap(off["k_a"] // kv_w)),
            pl.BlockSpec((seq_len, kv_w), seqmap(off["v_a"] // kv_w)),
            pl.BlockSpec((seq_len, LANES), seqmap(small + 1)),
            pl.BlockSpec((seq_len, LANES), seqmap(small + 2)),
        ],
        out_specs=pl.BlockSpec((tq, qa_w), lambda b, i: (b * nq + i, 0)),
        out_shape=jax.ShapeDtypeStruct((n_rows, qa_w), BF16),
        scratch_shapes=[pltpu.VMEM((n_ck, tk, tq), jnp.int32), pltpu.VMEM((n_ck, tk, tq), F32),
                        pltpu.VMEM((LANES, tq), F32), pltpu.VMEM((DSA_KV_HEADS * n_ck, VT_ROWS, tk), BF16),
                        pltpu.VMEM((DSA_KV_HEADS, DSA_HEADS // DSA_KV_HEADS * tq, DSA_HEAD_DIM), BF16)],
        compiler_params=_cparams("arbitrary", "arbitrary"),
    )(big, big, big, big, big, big, big)


def _conv_kernel(a_ref, g_ref, ha_ref, hg_ref, gate_ref, wdw_ref, bdw_ref, lng_ref, lnb_ref, wo_ref,
                 o_ref, y_ref, c_ref, z_ref, *, tt, rows, cols):
    t = pl.program_id(1)
    y_ref[CONV_HALO:, :] = a_ref[...].astype(F32) * jax.nn.sigmoid(g_ref[...].astype(F32))
    halo = ha_ref[...].astype(F32) * jax.nn.sigmoid(hg_ref[...].astype(F32))
    y_ref[:CONV_HALO, :] = jnp.where(t == 0, 0.0, halo)

    n_ch = y_ref.shape[1]
    lead = CONV_HALO - (CONV_WIDTH - 1)
    taps_of = [[j for j in range(CONV_WIDTH) if j % SUBLANES == b] for b in range(SUBLANES)]
    for c0 in range(0, n_ch, cols):
        cs = slice(c0, c0 + cols)
        for b, taps in enumerate(taps_of):
            n = tt + taps[-1] - b
            z_ref[b, :n, :] = y_ref[lead + b:lead + b + n, cs]

        def row_group(i, _, cs=cs):
            r0 = pl.multiple_of(i * rows, rows)
            acc = jnp.broadcast_to(bdw_ref[:, cs], (rows, cols))
            for b, taps in enumerate(taps_of):
                for j in taps:
                    acc = acc + wdw_ref[j:j + 1, cs] * z_ref[b, pl.ds(r0 + j - b, rows), :]
            c_ref[pl.ds(r0, rows), cs] = acc
            return 0

        lax.fori_loop(0, tt // rows, row_group, 0)

    x = c_ref[...]
    mu = jnp.mean(x, axis=-1, keepdims=True)
    xc = x - mu
    var = jnp.mean(xc * xc, axis=-1, keepdims=True)
    yn = xc * lax.rsqrt(var + NORM_EPS) * lng_ref[...] + lnb_ref[...]
    z = yn * jax.nn.sigmoid(yn)
    proj = jnp.dot(z.astype(BF16), wo_ref[...], preferred_element_type=F32)
    o_ref[...] = (gate_ref[...].astype(F32) * proj).astype(o_ref.dtype)


def _conv_branch(big, w_dw, b_dw, ln_g, ln_b, w_o, *, batch, seq_len, off, d_model, tt):
    n_rows = batch * seq_len
    n_ch = w_dw.shape[1]
    nt = seq_len // tt
    hb = tt // CONV_HALO
    w_pad = jnp.concatenate([w_dw.astype(F32), jnp.zeros((CONV_HALO - CONV_WIDTH, n_ch), F32)], axis=0)
    ca, cg, gc = off["conv_a"] // n_ch, off["conv_g"] // n_ch, off["gate"] // d_model + 2
    cur = lambda col: (lambda b, t: (b * nt + t, col))
    halo = lambda col: (lambda b, t: (jnp.maximum((b * nt + t) * hb - 1, 0), col))
    const = lambda b, t: (0, 0)
    vec = lambda v: v.reshape(1, n_ch).astype(F32)
    return pl.pallas_call(
        functools.partial(_conv_kernel, tt=tt, rows=CONV_ROWS, cols=min(CONV_COLS, n_ch)),
        grid=(batch, nt),
        in_specs=[
            pl.BlockSpec((tt, n_ch), cur(ca)),
            pl.BlockSpec((tt, n_ch), cur(cg)),
            pl.BlockSpec((CONV_HALO, n_ch), halo(ca)),
            pl.BlockSpec((CONV_HALO, n_ch), halo(cg)),
            pl.BlockSpec((tt, d_model), cur(gc)),
            pl.BlockSpec((CONV_HALO, n_ch), const),
            pl.BlockSpec((1, n_ch), const),
            pl.BlockSpec((1, n_ch), const),
            pl.BlockSpec((1, n_ch), const),
            pl.BlockSpec((n_ch, d_model), const, pipeline_mode=pl.Buffered(1)),
        ],
        out_specs=pl.BlockSpec((tt, d_model), lambda b, t: (b * nt + t, 0)),
        out_shape=jax.ShapeDtypeStruct((n_rows, d_model), F32),
        scratch_shapes=[pltpu.VMEM((tt + CONV_HALO, n_ch), F32), pltpu.VMEM((tt, n_ch), F32),
                        pltpu.VMEM((SUBLANES, tt + CONV_HALO - SUBLANES, min(CONV_COLS, n_ch)), F32)],
        compiler_params=_cparams("parallel", "arbitrary"),
    )(big, big, big, big, big, w_pad, vec(b_dw), vec(ln_g), vec(ln_b), w_o)


def _mlp_kernel(x_ref, g_ref, wu_ref, wd_ref, gf_ref, o_ref, u_ref, acc_ref, *, final_norm):
    f = pl.program_id(1)

    @pl.when(f == 0)
    def _():
        u_ref[...] = _rms(x_ref[...], g_ref[...]).astype(BF16)
        acc_ref[...] = jnp.zeros_like(acc_ref)

    h = jnp.maximum(jnp.dot(u_ref[...], wu_ref[...], preferred_element_type=F32), 0.0)
    acc_ref[...] += jnp.dot((h * h).astype(BF16), wd_ref[...], preferred_element_type=F32)

    @pl.when(f == pl.num_programs(1) - 1)
    def _():
        y = x_ref[...] + acc_ref[...]
        if final_norm:
            y = _rms(y, gf_ref[...])
        o_ref[...] = y


def _mlp(x, gain, w_up, w_down, final_gain, *, final_norm, tm, tf):
    n_rows, d_model = x.shape
    d_ff = w_up.shape[1]
    vec = lambda v: v.reshape(1, d_model).astype(F32)
    return pl.pallas_call(
        functools.partial(_mlp_kernel, final_norm=final_norm),
        grid=(n_rows // tm, d_ff // tf),
        in_specs=[
            pl.BlockSpec((tm, d_model), lambda i, f: (i, 0)),
            pl.BlockSpec((1, d_model), lambda i, f: (0, 0)),
            pl.BlockSpec((d_model, tf), lambda i, f: (0, f)),
            pl.BlockSpec((tf, d_model), lambda i, f: (f, 0)),
            pl.BlockSpec((1, d_model), lambda i, f: (0, 0)),
        ],
        out_specs=pl.BlockSpec((tm, d_model), lambda i, f: (i, 0)),
        out_shape=jax.ShapeDtypeStruct((n_rows, d_model), F32),
        scratch_shapes=[pltpu.VMEM((tm, d_model), BF16), pltpu.VMEM((tm, d_model), F32)],
        compiler_params=_cparams("parallel", "arbitrary"),
    )(x, vec(gain), w_up, w_down, vec(final_gain))


def _prep_w_in(w_in, d_model):
    widths = [MLA_Q_LORA, MLA_KV_LORA, MLA_ROPE, DSA_HEADS * DSA_HEAD_DIM, DSA_KV_HEADS * DSA_HEAD_DIM,
              DSA_KV_HEADS * DSA_HEAD_DIM, IDX_HEADS * IDX_HEAD_DIM, IDX_HEAD_DIM, IDX_HEADS,
              2 * d_model, N_BRANCH * d_model]
    splits = [int(v) for v in np.cumsum(widths)[:-1]]
    c_q, c_kv, k_rope, q_a, k_a, v_a, q_i, k_i, w_i, conv_in, gates = jnp.split(w_in, splits, axis=1)
    z = lambda n: jnp.zeros((w_in.shape[0], n), w_in.dtype)
    small = jnp.concatenate([k_rope, z(64), k_i, z(64), z(64), k_i, w_i, w_i, z(LANES - 2 * IDX_HEADS)], axis=1)
    return jnp.concatenate([q_a, q_i, c_q, c_kv, conv_in, gates, k_a, v_a, small], axis=1).astype(BF16)


def _prep_w_uq(w_uq):
    k = w_uq.shape[0]
    w = w_uq.reshape(k, MLA_HEADS, MLA_NOPE + MLA_ROPE)
    w = jnp.concatenate([w, jnp.zeros((k, MLA_HEADS, 2 * LANES - MLA_NOPE - MLA_ROPE), w.dtype)], axis=2)
    return w.reshape(k, MLA_HEADS * 2 * LANES).astype(BF16)


def _prep_w_ukv(w_ukv):
    k = w_ukv.shape[0]
    w = w_ukv.reshape(k, MLA_HEADS, 2, MLA_NOPE)
    return jnp.transpose(w, (0, 2, 1, 3)).reshape(k, 2 * MLA_HEADS * MLA_NOPE).astype(BF16)


def _tile_ranges(off):
    t = lambda name: off[name] // COL_TILE
    per_tile = COL_TILE // LANES
    dsa_scale = float(DSA_HEAD_DIM ** -0.5 * LOG2_E)
    idx_scale = float((IDX_HEADS * IDX_HEAD_DIM) ** -0.5)
    rope_q = (("rope", 1, dsa_scale),) * per_tile
    rope_k = (("rope", 1, 1.0),) * per_tile
    rope_idx = (("rope", 2, 1.0),) * per_tile
    plain = (("plain", None, 1.0),) * per_tile
    sigm = (("sigmoid", None, 1.0),) * per_tile
    small = (("rope", 0, 1.0), ("rope", 2, 1.0), ("rope", 2, 1.0), ("hilo", None, idx_scale))
    return (
        (rope_q, ((t("q_a"), t("q_i")),)),
        (rope_k, ((t("k_a"), t("v_a")),)),
        (rope_idx, ((t("q_i"), t("c_q")),)),
        ("rms", ((t("c_q"), t("conv_a")),)),
        (plain, ((t("conv_a"), t("gate")), (t("v_a"), t("small")))),
        (sigm, ((t("gate"), t("k_a")),)),
        (small, ((t("small"), t("total")),)),
    )


def _pick(n, pref):
    return pref if n % pref == 0 else n


def kernel(x, attn_norm, w_in, mla_q_norm, mla_kv_norm, mla_w_uq, mla_w_ukv, conv_w_dw, conv_b_dw,
           conv_ln_g, conv_ln_b, w_o_mla, w_o_dsa, w_o_conv, w_out, mlp_norm, w_up, w_down, final_norm):
    batch, seq_len, d_model = x.shape
    depth = w_in.shape[0]
    n_rows = batch * seq_len
    off = _in_layout(d_model)
    kinds = _tile_ranges(off)

    tab_mla, half_mla = _rope_lane_tables(seq_len, MLA_ROPE, (0,))
    tab_dsa, half_dsa = _rope_lane_tables(seq_len, DSA_ROT, (0,))
    tab_idx, half_idx = _rope_lane_tables(seq_len, IDX_ROT, (0, IDX_HEAD_DIM))
    tabs = (tab_mla, tab_dsa, tab_idx)
    halves = (half_mla, half_dsa, half_idx)

    mla_scale = float((MLA_NOPE + MLA_ROPE) ** -0.5 * LOG2_E)
    mla_cols = MLA_HEADS * 2 * LANES
    q_kind = (("plain", None, mla_scale), ("rope", 0, mla_scale)) * MLA_HEADS
    kv_kind = (("plain", None, 1.0),) * (mla_cols // LANES)
    everywhere = ((0, 1),)

    tm = _pick(seq_len, 512)
    xf = x.reshape(n_rows, d_model)
    for i in range(depth):
        ones = jnp.ones((off["total"],), F32)
        rms_gain = lax.dynamic_update_slice(ones, mla_q_norm[i].astype(F32), (off["c_q"],))
        rms_gain = lax.dynamic_update_slice(rms_gain, mla_kv_norm[i].astype(F32), (off["c_kv"],))
        big = _project(xf, _prep_w_in(w_in[i], d_model), gain=attn_norm[i], rms_gain=rms_gain.reshape(1, -1),
                       rope_tabs=tabs, rope_halves=halves, kinds=kinds, seq_len=seq_len,
                       tm=_pick(seq_len, 1024))

        q_mla = _project(big, _prep_w_uq(mla_w_uq[i]), a_col_block=off["c_q"] // MLA_Q_LORA,
                         rope_tabs=(tab_mla,), rope_halves=(half_mla,), kinds=((q_kind, everywhere),),
                         seq_len=seq_len, tm=tm, tn=mla_cols)
        kv_mla = _project(big, _prep_w_ukv(mla_w_ukv[i]), a_col_block=off["c_kv"] // MLA_KV_LORA,
                          kinds=((kv_kind, everywhere),), seq_len=seq_len, tm=tm, tn=mla_cols)
        o_mla = _mla_attention(q_mla, kv_mla, big, batch=batch, seq_len=seq_len,
                               kr_col_block=off["small"] // LANES, tq=_pick(seq_len, 1024), tk=512)
        o_dsa = _dsa_attention(big, batch=batch, seq_len=seq_len, off=off, tq=_pick(seq_len, 256), tk=512)

        gated_conv = _conv_branch(big, conv_w_dw[i], conv_b_dw[i], conv_ln_g[i], conv_ln_b[i],
                                  w_o_conv[i].astype(BF16), batch=batch, seq_len=seq_len, off=off,
                                  d_model=d_model, tt=_pick(seq_len, 256))
        xf = _merge_project(o_mla, o_dsa, big, gated_conv, xf, w_o_mla[i].astype(BF16),
                            w_o_dsa[i].astype(BF16), w_out[i].astype(BF16),
                            gate_col_block=off["gate"] // d_model, tm=_pick(seq_len, 256))
        xf = _mlp(xf, mlp_norm[i], w_up[i].astype(BF16), w_down[i].astype(BF16), final_norm,
                  final_norm=(i == depth - 1), tm=tm, tf=512)
    return xf.reshape(batch, seq_len, d_model)
```

```python
import functools

import jax
import jax.numpy as jnp
import numpy as np
from jax import lax
from jax.experimental import pallas as pl
from jax.experimental.pallas import tpu as pltpu

F32 = jnp.float32
BF16 = jnp.bfloat16

ROPE_THETA = 500000.0
NORM_EPS = 1e-6
N_BRANCH = 3

MLA_HEADS = 16
MLA_Q_LORA = 512
MLA_KV_LORA = 512
MLA_NOPE = 128
MLA_ROPE = 64
MLA_V = 128

DSA_HEADS = 16
DSA_KV_HEADS = 4
DSA_HEAD_DIM = 128
DSA_ROT = DSA_HEAD_DIM // 4
IDX_HEADS = 16
IDX_HEAD_DIM = 64
IDX_ROT = IDX_HEAD_DIM // 4
INDEX_TOPK = 256

CONV_WIDTH = 31

LANES = 128
SUBLANES = 8
COL_TILE = 512
CONV_HALO = 32
CONV_ROWS, CONV_COLS = 32, 512
MASK_VALUE = -1e30
LOG2_E = 1.4426950408889634
HEAD_V = 128
VT_ROWS = HEAD_V + 16
INT_MIN = -2 ** 31
VMEM_LIMIT = 56 * 1024 * 1024

_NT = (((1,), (1,)), ((), ()))


def _cparams(*sem):
    return pltpu.CompilerParams(dimension_semantics=sem, vmem_limit_bytes=VMEM_LIMIT)


def _in_layout(d_model):
    c = d_model
    off = {"q_a": 0, "q_i": 2048, "c_q": 3072, "c_kv": 3584, "conv_a": 4096, "conv_g": 4096 + c,
           "gate": 4096 + 2 * c}
    off["k_a"] = off["gate"] + N_BRANCH * d_model
    off["v_a"] = off["k_a"] + 512
    off["small"] = off["v_a"] + 512
    off["total"] = off["small"] + 512
    return off


def _rope_lane_tables(seq_len, rot_dim, lane_starts):
    half = rot_dim // 2
    inv = ROPE_THETA ** (-jnp.arange(0, rot_dim, 2, dtype=F32) / rot_dim)
    ang = jnp.arange(seq_len, dtype=F32)[:, None] * inv[None, :]
    cos, sin = jnp.cos(ang), jnp.sin(ang)
    ones = lambda n: jnp.ones((seq_len, n), F32)
    zeros = lambda n: jnp.zeros((seq_len, n), F32)
    c_parts, a_parts, b_parts = [], [], []
    pos = 0
    for s0 in lane_starts:
        gap = s0 - pos
        c_parts += [ones(gap), cos, cos]
        a_parts += [zeros(gap), -sin, zeros(half)]
        b_parts += [zeros(gap), zeros(half), sin]
        pos = s0 + 2 * half
    c_parts.append(ones(LANES - pos))
    a_parts.append(zeros(LANES - pos))
    b_parts.append(zeros(LANES - pos))
    cat = lambda p: jnp.concatenate(p, axis=1)
    return (cat(c_parts), cat(a_parts), cat(b_parts)), half


def _sigmoid(x):
    return 0.5 * jnp.tanh(0.5 * x) + 0.5


def _rms(x, gain):
    ms = jnp.mean(x * x, axis=-1, keepdims=True)
    return x * lax.rsqrt(ms + NORM_EPS) * gain


def _proj_kernel(*refs, norm, has_rms, rope_halves, kinds):
    it = iter(refs)
    a_ref = next(it)
    ga_ref = next(it) if norm else None
    w_ref = next(it)
    rg_ref = next(it) if has_rms else None
    tabs = [(next(it), next(it), next(it)) for _ in rope_halves]
    o_ref = next(it)
    u_ref = next(it) if norm else None
    j = pl.program_id(1)

    if norm:
        @pl.when(j == 0)
        def _():
            u_ref[...] = _rms(a_ref[...], ga_ref[...]).astype(BF16)

    def chunk_epilogue(op, x):
        name, arg, scale = op
        if name == "plain":
            y = x
        elif name == "sigmoid":
            y = _sigmoid(x)
        elif name == "rope":
            c_ref, sa_ref, sb_ref = tabs[arg]
            half = rope_halves[arg]
            y = (x * c_ref[...] + pltpu.roll(x, LANES - half, 1) * sa_ref[...]
                 + pltpu.roll(x, half, 1) * sb_ref[...])
        elif name == "hilo":
            v = x * scale
            lo = v - v.astype(o_ref.dtype).astype(F32)
            lane = lax.broadcasted_iota(jnp.int32, x.shape, 1)
            return jnp.where(lane < IDX_HEADS, v, lo)
        else:
            raise ValueError(op)
        return y if scale == 1.0 else y * scale

    def tile(kind):
        a = u_ref[...] if norm else a_ref[...]
        acc = jnp.dot(a, w_ref[...], preferred_element_type=F32)
        if kind == "rms":
            o_ref[...] = _rms(acc, rg_ref[...]).astype(o_ref.dtype)
            return
        for c, op in enumerate(kind):
            sl = slice(c * LANES, (c + 1) * LANES)
            o_ref[:, sl] = chunk_epilogue(op, acc[:, sl]).astype(o_ref.dtype)

    if len(kinds) == 1:
        tile(kinds[0][0])
    else:
        for kind, ranges in kinds:
            cond = None
            for lo, hi in ranges:
                c = (j >= lo) & (j < hi)
                cond = c if cond is None else (cond | c)
            pl.when(cond)(functools.partial(tile, kind))


def _project(a, w, *, a_col_block=0, gain=None, rms_gain=None, rope_tabs=(), rope_halves=(), kinds,
             seq_len, tm, tn=COL_TILE, out_dtype=BF16):
    n_rows = a.shape[0]
    k_dim, n_cols = w.shape
    norm = gain is not None
    has_rms = rms_gain is not None
    t_blocks = seq_len // tm
    in_specs = [pl.BlockSpec((tm, k_dim), lambda i, j: (i, a_col_block))]
    args = [a]
    if norm:
        in_specs.append(pl.BlockSpec((1, k_dim), lambda i, j: (0, 0)))
        args.append(gain.reshape(1, k_dim).astype(F32))
    in_specs.append(pl.BlockSpec((k_dim, tn), lambda i, j: (0, j)))
    args.append(w)
    if has_rms:
        in_specs.append(pl.BlockSpec((1, tn), lambda i, j: (0, j)))
        args.append(rms_gain)
    for tab in rope_tabs:
        for t in tab:
            in_specs.append(pl.BlockSpec((tm, LANES), lambda i, j: (i % t_blocks, 0)))
            args.append(t)
    scratch = [pltpu.VMEM((tm, k_dim), BF16)] if norm else []
    return pl.pallas_call(
        functools.partial(_proj_kernel, norm=norm, has_rms=has_rms, rope_halves=tuple(rope_halves),
                          kinds=kinds),
        grid=(n_rows // tm, n_cols // tn),
        in_specs=in_specs,
        out_specs=pl.BlockSpec((tm, tn), lambda i, j: (i, j)),
        out_shape=jax.ShapeDtypeStruct((n_rows, n_cols), out_dtype),
        scratch_shapes=scratch,
        compiler_params=_cparams("parallel", "arbitrary"),
    )(*args)


def _merge_kernel(oa_ref, ob_ref, ga_ref, gb_ref, mc_ref, x_ref, wa_ref, wb_ref, wo_ref, o_ref):
    ya = jnp.dot(oa_ref[...], wa_ref[...], preferred_element_type=F32)
    merged = mc_ref[...] + ga_ref[...].astype(F32) * ya
    yb = jnp.dot(ob_ref[...], wb_ref[...], preferred_element_type=F32)
    merged = merged + gb_ref[...].astype(F32) * yb
    o_ref[...] = x_ref[...] + jnp.dot(merged.astype(BF16), wo_ref[...], preferred_element_type=F32)


def _merge_project(o_mla, o_dsa, big, gated_conv, x, w_a, w_b, w_out, *, gate_col_block, tm):
    n_rows, d_model = x.shape
    row = lambda col: (lambda i: (i, col))
    resident = lambda w: pl.BlockSpec(w.shape, lambda i: (0, 0), pipeline_mode=pl.Buffered(1))
    return pl.pallas_call(
        _merge_kernel,
        grid=(n_rows // tm,),
        in_specs=[
            pl.BlockSpec((tm, o_mla.shape[1]), row(0)),
            pl.BlockSpec((tm, o_dsa.shape[1]), row(0)),
            pl.BlockSpec((tm, d_model), row(gate_col_block)),
            pl.BlockSpec((tm, d_model), row(gate_col_block + 1)),
            pl.BlockSpec((tm, d_model), row(0)),
            pl.BlockSpec((tm, d_model), row(0)),
            resident(w_a), resident(w_b), resident(w_out),
        ],
        out_specs=pl.BlockSpec((tm, d_model), row(0)),
        out_shape=jax.ShapeDtypeStruct((n_rows, d_model), F32),
        compiler_params=_cparams("parallel"),
    )(o_mla, o_dsa, big, big, gated_conv, x, w_a, w_b, w_out)


def _store_transposed_values(vt_ref, idx, v):
    r = lax.broadcasted_iota(jnp.int32, (HEAD_V, HEAD_V), 0)
    c = lax.broadcasted_iota(jnp.int32, (HEAD_V, HEAD_V), 1)
    eye = jnp.where(r == c, 1.0, 0.0).astype(BF16)
    vt_ref[idx, :HEAD_V, :] = lax.dot_general(eye, v, _NT, preferred_element_type=F32).astype(BF16)
    vt_ref[idx, HEAD_V:, :] = jnp.ones((VT_ROWS - HEAD_V, v.shape[0]), BF16)


def _softmax_step(s, m, acc, vt):
    m_new = jnp.maximum(m, jnp.max(s, axis=0, keepdims=True))
    alpha = jnp.exp2(m - m_new)
    p = jnp.exp2(s - m_new).astype(BF16)
    return m_new, alpha * acc + jnp.dot(vt, p, preferred_element_type=F32)


def _softmax_init(tq):
    return jnp.full((1, tq), MASK_VALUE, F32), jnp.zeros((VT_ROWS, tq), F32)


def _softmax_result(acc):
    return jnp.transpose(acc[:HEAD_V] / acc[HEAD_V:HEAD_V + 1])


def _pipeline_buffers(tk, width):
    return [pltpu.VMEM((tk, width), F32)] * 2 + [pltpu.VMEM((tk, width), BF16)] * 2


def _softmax_pairs(n_pairs, qk, bias, vt_at, bufs, width, tail_masks=None):
    s_a, s_b, p_a, p_b = bufs
    n_blocks = 2 * n_pairs + (2 if tail_masks else 0)
    last = jnp.maximum(n_blocks - 1, 0)
    s_a[...] = qk(0)
    p_b[...] = jnp.zeros(p_b.shape, BF16)

    def stage(c, carry, s_cur, s_next, p_cur, p_prev, mask=None):
        m, alpha_prev, acc = carry
        s_next[...] = qk(jnp.minimum(c + 1, last))
        acc = alpha_prev * acc + jnp.dot(vt_at(jnp.maximum(c - 1, 0)), p_prev[...], preferred_element_type=F32)
        s = s_cur[...] if bias is None else s_cur[...] + bias(c)
        if mask is not None:
            s = mask(s)
        m_new = jnp.maximum(m, jnp.max(s, axis=0, keepdims=True))
        p_cur[...] = jnp.exp2(s - m_new).astype(BF16)
        return m_new, jnp.exp2(m - m_new), acc

    def pair(k, carry, masks=(None, None)):
        carry = stage(2 * k, carry, s_a, s_b, p_a, p_b, masks[0])
        return stage(2 * k + 1, carry, s_b, s_a, p_b, p_a, masks[1])

    m, acc = _softmax_init(width)
    carry = lax.fori_loop(0, n_pairs, pair, (m, jnp.ones((1, width), F32), acc))
    if tail_masks:
        carry = pair(n_pairs, carry, tail_masks)
    m, alpha_prev, acc = carry
    return m, alpha_prev * acc + jnp.dot(vt_at(last), p_b[...], preferred_element_type=F32)


def _mla_flash_kernel(q_ref, kn_ref, v_ref, kr_ref, o_ref, kcat_ref, vt_ref, *bufs, tq, tk):
    qi = pl.program_id(2)

    @pl.when(qi == 0)
    def _():
        kcat_ref[:, :LANES] = kn_ref[...]
        kcat_ref[:, LANES:] = kr_ref[...]
        for c in range(vt_ref.shape[0]):
            _store_transposed_values(vt_ref, c, v_ref[c * tk:(c + 1) * tk, :])

    q = q_ref[...]

    def scores(j):
        off = pl.multiple_of(j * tk, tk)
        return lax.dot_general(kcat_ref[pl.ds(off, tk), :], q, _NT, preferred_element_type=F32)

    def causal(d):
        def mask(s):
            key_pos = d * tk + lax.broadcasted_iota(jnp.int32, s.shape, 0)
            query_pos = lax.broadcasted_iota(jnp.int32, s.shape, 1)
            return jnp.where(key_pos <= query_pos, s, MASK_VALUE)
        return mask

    _, acc = _softmax_pairs(qi, scores, None, lambda j: vt_ref[j], bufs, tq, tail_masks=(causal(0), causal(1)))
    o_ref[...] = _softmax_result(acc).astype(o_ref.dtype)


def _mla_attention(q, kv, big, *, batch, seq_len, kr_col_block, tq, tk):
    assert tq == 2 * tk
    n_rows = batch * seq_len
    nq = seq_len // tq
    return pl.pallas_call(
        functools.partial(_mla_flash_kernel, tq=tq, tk=tk),
        grid=(batch, MLA_HEADS, nq),
        in_specs=[
            pl.BlockSpec((tq, 2 * LANES), lambda b, h, i: (b * nq + i, h)),
            pl.BlockSpec((seq_len, LANES), lambda b, h, i: (b, h)),
            pl.BlockSpec((seq_len, LANES), lambda b, h, i: (b, MLA_HEADS + h)),
            pl.BlockSpec((seq_len, LANES), lambda b, h, i: (b, kr_col_block)),
        ],
        out_specs=pl.BlockSpec((tq, MLA_V), lambda b, h, i: (b * nq + i, h)),
        out_shape=jax.ShapeDtypeStruct((n_rows, MLA_HEADS * MLA_V), BF16),
        scratch_shapes=[pltpu.VMEM((seq_len, 2 * LANES), BF16), pltpu.VMEM((seq_len // tk, VT_ROWS, tk), BF16),
                        *_pipeline_buffers(tk, tq)],
        compiler_params=_cparams("arbitrary", "arbitrary", "arbitrary"),
    )(q, kv, kv, big)


def _dsa_kernel(qa_ref, qi_ref, w_ref, ka_ref, va_ref, ke_ref, ko_ref, o_ref,
                key_ref, bias_ref, wt_ref, vt_ref, qs_ref, kg_ref, oacc_ref, *bufs, tq, tk, top_k, seq_bits):
    i = pl.program_id(1)
    n_chunks = (i * tq + tq + tk - 1) // tk
    n_groups = tk // SUBLANES
    query_pos = i * tq + lax.broadcasted_iota(jnp.int32, (tk, tq), 1)
    key_in_chunk = lax.broadcasted_iota(jnp.int32, (tk, tq), 0)
    key_in_group = lax.broadcasted_iota(jnp.int32, (SUBLANES, tq), 0)

    @pl.when(i == 0)
    def _():
        for g in range(DSA_KV_HEADS):
            kg_ref[g] = ka_ref[:, g * DSA_HEAD_DIM:(g + 1) * DSA_HEAD_DIM]
            for c in range(key_ref.shape[0]):
                v = va_ref[c * tk:(c + 1) * tk, g * DSA_HEAD_DIM:(g + 1) * DSA_HEAD_DIM]
                _store_transposed_values(vt_ref, g * key_ref.shape[0] + c, v)

    wf = w_ref[...].astype(F32)
    wt_ref[...] = jnp.transpose(wf + pltpu.roll(wf, LANES - IDX_HEADS, 1))

    grp = DSA_HEADS // DSA_KV_HEADS
    for h in range(DSA_HEADS):
        qs_ref[h // grp, (h % grp) * tq:(h % grp + 1) * tq, :] = qa_ref[:, h * DSA_HEAD_DIM:(h + 1) * DSA_HEAD_DIM]

    def score_chunk(c, _):
        off = pl.multiple_of(c * tk, tk)
        ke = ke_ref[pl.ds(off, tk), :]
        ko = ko_ref[pl.ds(off, tk), :]
        sc = jnp.zeros((tk, tq), F32)
        for ch in range(IDX_HEADS // 2):
            qc = qi_ref[:, ch * LANES:(ch + 1) * LANES]
            de = lax.dot_general(ke, qc, _NT, preferred_element_type=F32)
            do = lax.dot_general(ko, qc, _NT, preferred_element_type=F32)
            sc = (sc + wt_ref[2 * ch:2 * ch + 1, :] * jnp.maximum(de, 0.0)
                  + wt_ref[2 * ch + 1:2 * ch + 2, :] * jnp.maximum(do, 0.0))
        sc = jnp.where(sc == 0.0, 0.0, sc)
        bits = pltpu.bitcast(sc, jnp.int32)
        key = jnp.where(bits < 0, bits ^ jnp.int32(0x7FFFFFFF), bits)
        key_ref[c] = jnp.where(off + key_in_chunk <= query_pos, key, jnp.int32(INT_MIN))
        return 0

    lax.fori_loop(0, n_chunks, score_chunk, 0)

    def count(pred):
        def body(c, parts):
            parts = list(parts)
            for r in range(n_groups):
                key = key_ref[c, r * SUBLANES:(r + 1) * SUBLANES, :]
                hit = jnp.where(pred(key, c * tk + r * SUBLANES + key_in_group), 1, 0)
                parts[r % len(parts)] = parts[r % len(parts)] + hit
            return tuple(parts)
        parts = lax.fori_loop(0, n_chunks, body, (jnp.zeros((SUBLANES, tq), jnp.int32),) * 4)
        return jnp.sum(parts[0] + parts[1] + parts[2] + parts[3], axis=0, keepdims=True)

    zero = jnp.zeros((1, tq), jnp.int32)
    thr = jnp.where(count(lambda key, col: key >= zero) >= top_k, zero, jnp.int32(INT_MIN))

    def search(t, thr):
        cand = thr | (jnp.int32(1) << (30 - t))
        return jnp.where(count(lambda key, col: key >= cand) >= top_k, cand, thr)

    thr = lax.fori_loop(0, 31, search, thr)
    thr = jnp.maximum(thr, jnp.int32(INT_MIN + 1))

    need = top_k - count(lambda key, col: key > thr)
    n_eq = count(lambda key, col: key == thr)
    has_excess = jnp.max(jnp.where(n_eq > need, 1, 0)) > 0

    def tie_cut():
        def step(t, cut):
            cand = cut | (jnp.int32(1) << (seq_bits - 1 - t))
            cnt = count(lambda key, col: (key == thr) & (col < cand))
            return jnp.where(cnt < need, cand, cut)
        return lax.fori_loop(0, seq_bits, step, zero)

    cut = lax.cond(has_excess, tie_cut, lambda: jnp.full((1, tq), 2 ** seq_bits, jnp.int32))

    def bias_chunk(c, _):
        key = key_ref[c]
        sel = (key > thr) | ((key == thr) & (c * tk + key_in_chunk <= cut))
        bias_ref[c] = jnp.where(sel, 0.0, MASK_VALUE)
        return 0

    lax.fori_loop(0, n_chunks, bias_chunk, 0)

    s_a, s_b, p_a, p_b = bufs
    width = grp * tq
    n_ck = key_ref.shape[0]

    def scores(g, c):
        off = pl.multiple_of(c * tk, tk)
        return lax.dot_general(kg_ref[g, pl.ds(off, tk), :], qs_ref[g], _NT, preferred_element_type=F32)

    def advance(g, c):
        wrap = c + 1 >= n_chunks
        g1, c1 = jnp.where(wrap, g + 1, g), jnp.where(wrap, 0, c + 1)
        done = g1 >= DSA_KV_HEADS
        return jnp.where(done, g, g1), jnp.where(done, c, c1)

    def stage(carry, s_cur, s_next, p_cur, p_prev):
        m, alpha_prev, acc, g, c, g_prev, c_prev = carry
        g_next, c_next = advance(g, c)
        s_next[...] = scores(g_next, c_next)
        acc = alpha_prev * acc + jnp.dot(vt_ref[g_prev * n_ck + c_prev], p_prev[...], preferred_element_type=F32)
        oacc_ref[g_prev] = acc
        m = jnp.where(c == 0, MASK_VALUE, m)
        s = s_cur[...] + jnp.concatenate([bias_ref[c]] * grp, axis=1)
        m_new = jnp.maximum(m, jnp.max(s, axis=0, keepdims=True))
        p_cur[...] = jnp.exp2(s - m_new).astype(BF16)
        return m_new, jnp.exp2(m - m_new), acc, g_next, c_next, g, c

    def pair(_, carry):
        return stage(stage(carry, s_a, s_b, p_a, p_b), s_b, s_a, p_b, p_a)

    s_a[...] = scores(0, 0)
    p_b[...] = jnp.zeros(p_b.shape, BF16)
    m0, acc0 = _softmax_init(width)
    zero = jnp.int32(0)
    carry = lax.fori_loop(0, DSA_KV_HEADS // 2 * n_chunks, pair,
                          (m0, jnp.ones((1, width), F32), acc0, zero, zero, zero, zero))
    _, alpha_prev, acc, _, _, g_prev, c_prev = carry
    oacc_ref[DSA_KV_HEADS - 1] = alpha_prev * acc + jnp.dot(vt_ref[g_prev * n_ck + c_prev], p_b[...],
                                                           preferred_element_type=F32)
    for g in range(DSA_KV_HEADS):
        out = oacc_ref[g, :HEAD_V, :] / oacc_ref[g, HEAD_V:HEAD_V + 1, :]
        for r in range(grp):
            h = g * grp + r
            o_ref[:, h * DSA_HEAD_DIM:(h + 1) * DSA_HEAD_DIM] = jnp.transpose(
                out[:, r * tq:(r + 1) * tq]).astype(o_ref.dtype)


def _dsa_attention(big, *, batch, seq_len, off, tq, tk):
    n_rows = batch * seq_len
    nq = seq_len // tq
    top_k = min(INDEX_TOPK, seq_len // 4)
    seq_bits = int(seq_len).bit_length() - 1
    assert 2 ** seq_bits == seq_len and tk >= top_k
    qa_w, qi_w, kv_w = DSA_HEADS * DSA_HEAD_DIM, IDX_HEADS * IDX_HEAD_DIM, DSA_KV_HEADS * DSA_HEAD_DIM
    small = off["small"] // LANES
    rowmap = lambda col: (lambda b, i: (b * nq + i, col))
    seqmap = lambda col: (lambda b, i: (b, col))
    n_ck = seq_len // tk
    return pl.pallas_call(
        functools.partial(_dsa_kernel, tq=tq, tk=tk, top_k=top_k, seq_bits=seq_bits),
        grid=(batch, nq),
        in_specs=[
            pl.BlockSpec((tq, qa_w), rowmap(off["q_a"] // qa_w)),
            pl.BlockSpec((tq, qi_w), rowmap(off["q_i"] // qi_w)),
            pl.BlockSpec((tq, LANES), rowmap(small + 3)),
            pl.BlockSpec((seq_len, kv_w), seqmap(off["k_a"] // kv_w), pipeline_mode=pl.Buffered(1)),
            pl.BlockSpec((seq_len, kv_w), seqmap(off["v_a"] // kv_w), pipeline_mode=pl.Buffered(1)),
            pl.BlockSpec((seq_len, LANES), seqmap(small + 1), pipeline_mode=pl.Buffered(1)),
            pl.BlockSpec((seq_len, LANES), seqmap(small + 2), pipeline_mode=pl.Buffered(1)),
        ],
        out_specs=pl.BlockSpec((tq, qa_w), lambda b, i: (b * nq + i, 0)),
        out_shape=jax.ShapeDtypeStruct((n_rows, qa_w), BF16),
        scratch_shapes=[pltpu.VMEM((n_ck, tk, tq), jnp.int32), pltpu.VMEM((n_ck, tk, tq), F32),
                        pltpu.VMEM((LANES, tq), F32), pltpu.VMEM((DSA_KV_HEADS * n_ck, VT_ROWS, tk), BF16),
                        pltpu.VMEM((DSA_KV_HEADS, DSA_HEADS // DSA_KV_HEADS * tq, DSA_HEAD_DIM), BF16),
                        pltpu.VMEM((DSA_KV_HEADS, seq_len, DSA_HEAD_DIM), BF16),
                        pltpu.VMEM((DSA_KV_HEADS, VT_ROWS, DSA_HEADS // DSA_KV_HEADS * tq), F32),
                        *_pipeline_buffers(tk, DSA_HEADS // DSA_KV_HEADS * tq)],
        compiler_params=_cparams("arbitrary", "arbitrary"),
    )(big, big, big, big, big, big, big)


def _conv_kernel(a_ref, g_ref, ha_ref, hg_ref, gate_ref, wdw_ref, bdw_ref, lng_ref, lnb_ref, wo_ref,
                 o_ref, y_ref, c_ref, z_ref, *, tt, rows, cols):
    t = pl.program_id(1)
    y_ref[CONV_HALO:, :] = a_ref[...].astype(F32) * _sigmoid(g_ref[...].astype(F32))
    halo = ha_ref[...].astype(F32) * _sigmoid(hg_ref[...].astype(F32))
    y_ref[:CONV_HALO, :] = jnp.where(t == 0, 0.0, halo)

    n_ch = y_ref.shape[1]
    lead = CONV_HALO - (CONV_WIDTH - 1)
    taps_of = [[j for j in range(CONV_WIDTH) if j % SUBLANES == b] for b in range(SUBLANES)]
    for c0 in range(0, n_ch, cols):
        cs = slice(c0, c0 + cols)
        for b, taps in enumerate(taps_of):
            n = tt + taps[-1] - b
            z_ref[b, :n, :] = y_ref[lead + b:lead + b + n, cs]

        def row_group(i, _, cs=cs):
            r0 = pl.multiple_of(i * rows, rows)
            acc = jnp.broadcast_to(bdw_ref[:, cs], (rows, cols))
            for b, taps in enumerate(taps_of):
                for j in taps:
                    acc = acc + wdw_ref[j:j + 1, cs] * z_ref[b, pl.ds(r0 + j - b, rows), :]
            c_ref[pl.ds(r0, rows), cs] = acc
            return 0

        lax.fori_loop(0, tt // rows, row_group, 0)

    x = c_ref[...]
    mu = jnp.mean(x, axis=-1, keepdims=True)
    xc = x - mu
    var = jnp.mean(xc * xc, axis=-1, keepdims=True)
    yn = xc * lax.rsqrt(var + NORM_EPS) * lng_ref[...] + lnb_ref[...]
    z = yn * _sigmoid(yn)
    proj = jnp.dot(z.astype(BF16), wo_ref[...], preferred_element_type=F32)
    o_ref[...] = (gate_ref[...].astype(F32) * proj).astype(o_ref.dtype)


def _conv_branch(big, w_dw, b_dw, ln_g, ln_b, w_o, *, batch, seq_len, off, d_model, tt):
    n_rows = batch * seq_len
    n_ch = w_dw.shape[1]
    nt = seq_len // tt
    hb = tt // CONV_HALO
    w_pad = jnp.concatenate([w_dw.astype(F32), jnp.zeros((CONV_HALO - CONV_WIDTH, n_ch), F32)], axis=0)
    ca, cg, gc = off["conv_a"] // n_ch, off["conv_g"] // n_ch, off["gate"] // d_model + 2
    cur = lambda col: (lambda b, t: (b * nt + t, col))
    halo = lambda col: (lambda b, t: (jnp.maximum((b * nt + t) * hb - 1, 0), col))
    const = lambda b, t: (0, 0)
    vec = lambda v: v.reshape(1, n_ch).astype(F32)
    return pl.pallas_call(
        functools.partial(_conv_kernel, tt=tt, rows=CONV_ROWS, cols=min(CONV_COLS, n_ch)),
        grid=(batch, nt),
        in_specs=[
            pl.BlockSpec((tt, n_ch), cur(ca)),
            pl.BlockSpec((tt, n_ch), cur(cg)),
            pl.BlockSpec((CONV_HALO, n_ch), halo(ca)),
            pl.BlockSpec((CONV_HALO, n_ch), halo(cg)),
            pl.BlockSpec((tt, d_model), cur(gc)),
            pl.BlockSpec((CONV_HALO, n_ch), const),
            pl.BlockSpec((1, n_ch), const),
            pl.BlockSpec((1, n_ch), const),
            pl.BlockSpec((1, n_ch), const),
            pl.BlockSpec((n_ch, d_model), const, pipeline_mode=pl.Buffered(1)),
        ],
        out_specs=pl.BlockSpec((tt, d_model), lambda b, t: (b * nt + t, 0)),
        out_shape=jax.ShapeDtypeStruct((n_rows, d_model), F32),
        scratch_shapes=[pltpu.VMEM((tt + CONV_HALO, n_ch), F32), pltpu.VMEM((tt, n_ch), F32),
                        pltpu.VMEM((SUBLANES, tt + CONV_HALO - SUBLANES, min(CONV_COLS, n_ch)), F32)],
        compiler_params=_cparams("parallel", "arbitrary"),
    )(big, big, big, big, big, w_pad, vec(b_dw), vec(ln_g), vec(ln_b), w_o)


def _mlp_kernel(x_ref, g_ref, wu_ref, wd_ref, gf_ref, o_ref, u_ref, acc_ref, *, final_norm):
    f = pl.program_id(1)

    @pl.when(f == 0)
    def _():
        u_ref[...] = _rms(x_ref[...], g_ref[...]).astype(BF16)
        acc_ref[...] = jnp.zeros_like(acc_ref)

    h = jnp.maximum(jnp.dot(u_ref[...], wu_ref[...], preferred_element_type=F32), 0.0)
    acc_ref[...] += jnp.dot((h * h).astype(BF16), wd_ref[...], preferred_element_type=F32)

    @pl.when(f == pl.num_programs(1) - 1)
    def _():
        y = x_ref[...] + acc_ref[...]
        if final_norm:
            y = _rms(y, gf_ref[...])
        o_ref[...] = y


def _mlp(x, gain, w_up, w_down, final_gain, *, final_norm, tm, tf):
    n_rows, d_model = x.shape
    d_ff = w_up.shape[1]
    vec = lambda v: v.reshape(1, d_model).astype(F32)
    return pl.pallas_call(
        functools.partial(_mlp_kernel, final_norm=final_norm),
        grid=(n_rows // tm, d_ff // tf),
        in_specs=[
            pl.BlockSpec((tm, d_model), lambda i, f: (i, 0)),
            pl.BlockSpec((1, d_model), lambda i, f: (0, 0)),
            pl.BlockSpec((d_model, tf), lambda i, f: (0, f)),
            pl.BlockSpec((tf, d_model), lambda i, f: (f, 0)),
            pl.BlockSpec((1, d_model), lambda i, f: (0, 0)),
        ],
        out_specs=pl.BlockSpec((tm, d_model), lambda i, f: (i, 0)),
        out_shape=jax.ShapeDtypeStruct((n_rows, d_model), F32),
        scratch_shapes=[pltpu.VMEM((tm, d_model), BF16), pltpu.VMEM((tm, d_model), F32)],
        compiler_params=_cparams("parallel", "arbitrary"),
    )(x, vec(gain), w_up, w_down, vec(final_gain))


def _prep_w_in(w_in, d_model):
    widths = [MLA_Q_LORA, MLA_KV_LORA, MLA_ROPE, DSA_HEADS * DSA_HEAD_DIM, DSA_KV_HEADS * DSA_HEAD_DIM,
              DSA_KV_HEADS * DSA_HEAD_DIM, IDX_HEADS * IDX_HEAD_DIM, IDX_HEAD_DIM, IDX_HEADS,
              2 * d_model, N_BRANCH * d_model]
    splits = [int(v) for v in np.cumsum(widths)[:-1]]
    c_q, c_kv, k_rope, q_a, k_a, v_a, q_i, k_i, w_i, conv_in, gates = jnp.split(w_in, splits, axis=1)
    z = lambda n: jnp.zeros((w_in.shape[0], n), w_in.dtype)
    small = jnp.concatenate([k_rope, z(64), k_i, z(64), z(64), k_i, w_i, w_i, z(LANES - 2 * IDX_HEADS)], axis=1)
    return jnp.concatenate([q_a, q_i, c_q, c_kv, conv_in, gates, k_a, v_a, small], axis=1).astype(BF16)


def _prep_w_uq(w_uq):
    k = w_uq.shape[0]
    w = w_uq.reshape(k, MLA_HEADS, MLA_NOPE + MLA_ROPE)
    w = jnp.concatenate([w, jnp.zeros((k, MLA_HEADS, 2 * LANES - MLA_NOPE - MLA_ROPE), w.dtype)], axis=2)
    return w.reshape(k, MLA_HEADS * 2 * LANES).astype(BF16)


def _prep_w_ukv(w_ukv):
    k = w_ukv.shape[0]
    w = w_ukv.reshape(k, MLA_HEADS, 2, MLA_NOPE)
    return jnp.transpose(w, (0, 2, 1, 3)).reshape(k, 2 * MLA_HEADS * MLA_NOPE).astype(BF16)


def _tile_ranges(off):
    t = lambda name: off[name] // COL_TILE
    per_tile = COL_TILE // LANES
    dsa_scale = float(DSA_HEAD_DIM ** -0.5 * LOG2_E)
    idx_scale = float((IDX_HEADS * IDX_HEAD_DIM) ** -0.5)
    rope_q = (("rope", 1, dsa_scale),) * per_tile
    rope_k = (("rope", 1, 1.0),) * per_tile
    rope_idx = (("rope", 2, 1.0),) * per_tile
    plain = (("plain", None, 1.0),) * per_tile
    sigm = (("sigmoid", None, 1.0),) * per_tile
    small = (("rope", 0, 1.0), ("rope", 2, 1.0), ("rope", 2, 1.0), ("hilo", None, idx_scale))
    return (
        (rope_q, ((t("q_a"), t("q_i")),)),
        (rope_k, ((t("k_a"), t("v_a")),)),
        (rope_idx, ((t("q_i"), t("c_q")),)),
        ("rms", ((t("c_q"), t("conv_a")),)),
        (plain, ((t("conv_a"), t("gate")), (t("v_a"), t("small")))),
        (sigm, ((t("gate"), t("k_a")),)),
        (small, ((t("small"), t("total")),)),
    )


def _pick(n, pref):
    return pref if n % pref == 0 else n


def kernel(x, attn_norm, w_in, mla_q_norm, mla_kv_norm, mla_w_uq, mla_w_ukv, conv_w_dw, conv_b_dw,
           conv_ln_g, conv_ln_b, w_o_mla, w_o_dsa, w_o_conv, w_out, mlp_norm, w_up, w_down, final_norm):
    batch, seq_len, d_model = x.shape
    depth = w_in.shape[0]
    n_rows = batch * seq_len
    off = _in_layout(d_model)
    kinds = _tile_ranges(off)

    tab_mla, half_mla = _rope_lane_tables(seq_len, MLA_ROPE, (0,))
    tab_dsa, half_dsa = _rope_lane_tables(seq_len, DSA_ROT, (0,))
    tab_idx, half_idx = _rope_lane_tables(seq_len, IDX_ROT, (0, IDX_HEAD_DIM))
    tabs = (tab_mla, tab_dsa, tab_idx)
    halves = (half_mla, half_dsa, half_idx)

    mla_scale = float((MLA_NOPE + MLA_ROPE) ** -0.5 * LOG2_E)
    mla_cols = MLA_HEADS * 2 * LANES
    q_kind = (("plain", None, mla_scale), ("rope", 0, mla_scale)) * MLA_HEADS
    kv_kind = (("plain", None, 1.0),) * (mla_cols // LANES)
    everywhere = ((0, 1),)

    tm = _pick(seq_len, 512)
    xf = x.reshape(n_rows, d_model)
    for i in range(depth):
        ones = jnp.ones((off["total"],), F32)
        rms_gain = lax.dynamic_update_slice(ones, mla_q_norm[i].astype(F32), (off["c_q"],))
        rms_gain = lax.dynamic_update_slice(rms_gain, mla_kv_norm[i].astype(F32), (off["c_kv"],))
        big = _project(xf, _prep_w_in(w_in[i], d_model), gain=attn_norm[i], rms_gain=rms_gain.reshape(1, -1),
                       rope_tabs=tabs, rope_halves=halves, kinds=kinds, seq_len=seq_len,
                       tm=_pick(seq_len, 1024))

        q_mla = _project(big, _prep_w_uq(mla_w_uq[i]), a_col_block=off["c_q"] // MLA_Q_LORA,
                         rope_tabs=(tab_mla,), rope_halves=(half_mla,), kinds=((q_kind, everywhere),),
                         seq_len=seq_len, tm=tm, tn=mla_cols)
        kv_mla = _project(big, _prep_w_ukv(mla_w_ukv[i]), a_col_block=off["c_kv"] // MLA_KV_LORA,
                          kinds=((kv_kind, everywhere),), seq_len=seq_len, tm=tm, tn=mla_cols)
        o_mla = _mla_attention(q_mla, kv_mla, big, batch=batch, seq_len=seq_len,
                               kr_col_block=off["small"] // LANES, tq=_pick(seq_len, 1024), tk=512)
        o_dsa = _dsa_attention(big, batch=batch, seq_len=seq_len, off=off, tq=_pick(seq_len, 256), tk=512)

        gated_conv = _conv_branch(big, conv_w_dw[i], conv_b_dw[i], conv_ln_g[i], conv_ln_b[i],
                                  w_o_conv[i].astype(BF16), batch=batch, seq_len=seq_len, off=off,
                                  d_model=d_model, tt=_pick(seq_len, 256))
        xf = _merge_project(o_mla, o_dsa, big, gated_conv, xf, w_o_mla[i].astype(BF16),
                            w_o_dsa[i].astype(BF16), w_out[i].astype(BF16),
                            gate_col_block=off["gate"] // d_model, tm=_pick(seq_len, 256))
        xf = _mlp(xf, mlp_norm[i], w_up[i].astype(BF16), w_down[i].astype(BF16), final_norm,
                  final_norm=(i == depth - 1), tm=tm, tf=512)
    return xf.reshape(batch, seq_len, d_model)
```

```python
import functools

import jax
import jax.numpy as jnp
import numpy as np
from jax import lax
from jax.experimental import pallas as pl
from jax.experimental.pallas import tpu as pltpu

F32 = jnp.float32
BF16 = jnp.bfloat16

ROPE_THETA = 500000.0
NORM_EPS = 1e-6
N_BRANCH = 3

MLA_HEADS = 16
MLA_Q_LORA = 512
MLA_KV_LORA = 512
MLA_NOPE = 128
MLA_ROPE = 64
MLA_V = 128

DSA_HEADS = 16
DSA_KV_HEADS = 4
DSA_HEAD_DIM = 128
DSA_ROT = DSA_HEAD_DIM // 4
IDX_HEADS = 16
IDX_HEAD_DIM = 64
IDX_ROT = IDX_HEAD_DIM // 4
INDEX_TOPK = 256

CONV_WIDTH = 31

LANES = 128
SUBLANES = 8
COL_TILE = 512
PROJ_PART_ROWS = 256
CONV_HALO = 32
CONV_ROWS, CONV_COLS = 32, 512
MASK_VALUE = -1e30
LOG2_E = 1.4426950408889634
HEAD_V = 128
VT_ROWS = HEAD_V + 16
INT_MIN = -2 ** 31
VMEM_LIMIT = 56 * 1024 * 1024

_NT = (((1,), (1,)), ((), ()))


def _cparams(*sem):
    return pltpu.CompilerParams(dimension_semantics=sem, vmem_limit_bytes=VMEM_LIMIT)


def _in_layout(d_model):
    c = d_model
    off = {"q_a": 0, "q_i": 2048, "c_q": 3072, "c_kv": 3584, "conv_a": 4096, "conv_g": 4096 + c,
           "gate": 4096 + 2 * c}
    off["k_a"] = off["gate"] + N_BRANCH * d_model
    off["v_a"] = off["k_a"] + 512
    off["small"] = off["v_a"] + 512
    off["total"] = off["small"] + 512
    return off


def _rope_lane_tables(seq_len, rot_dim, lane_starts):
    half = rot_dim // 2
    inv = ROPE_THETA ** (-jnp.arange(0, rot_dim, 2, dtype=F32) / rot_dim)
    ang = jnp.arange(seq_len, dtype=F32)[:, None] * inv[None, :]
    cos, sin = jnp.cos(ang), jnp.sin(ang)
    ones = lambda n: jnp.ones((seq_len, n), F32)
    zeros = lambda n: jnp.zeros((seq_len, n), F32)
    c_parts, a_parts, b_parts = [], [], []
    pos = 0
    for s0 in lane_starts:
        gap = s0 - pos
        c_parts += [ones(gap), cos, cos]
        a_parts += [zeros(gap), -sin, zeros(half)]
        b_parts += [zeros(gap), zeros(half), sin]
        pos = s0 + 2 * half
    c_parts.append(ones(LANES - pos))
    a_parts.append(zeros(LANES - pos))
    b_parts.append(zeros(LANES - pos))
    cat = lambda p: jnp.concatenate(p, axis=1)
    return (cat(c_parts), cat(a_parts), cat(b_parts)), half


def _sigmoid(x):
    return 0.5 * jnp.tanh(0.5 * x) + 0.5


def _rms(x, gain):
    ms = jnp.mean(x * x, axis=-1, keepdims=True)
    return x * lax.rsqrt(ms + NORM_EPS) * gain


def _proj_kernel(*refs, norm, has_rms, rope_halves, kinds, row_parts):
    it = iter(refs)
    a_ref = next(it)
    ga_ref = next(it) if norm else None
    w_ref = next(it)
    rg_ref = next(it) if has_rms else None
    tabs = [(next(it), next(it), next(it)) for _ in rope_halves]
    o_ref = next(it)
    u_ref = next(it) if norm else None
    j = pl.program_id(1)

    if norm:
        @pl.when(j == 0)
        def _():
            u_ref[...] = _rms(a_ref[...], ga_ref[...]).astype(BF16)

    def chunk_epilogue(op, x, rows):
        name, arg, scale = op
        if name == "plain":
            y = x
        elif name == "sigmoid":
            y = _sigmoid(x)
        elif name == "rope":
            c_ref, sa_ref, sb_ref = tabs[arg]
            half = rope_halves[arg]
            y = (x * c_ref[rows, :] + pltpu.roll(x, LANES - half, 1) * sa_ref[rows, :]
                 + pltpu.roll(x, half, 1) * sb_ref[rows, :])
        elif name == "hilo":
            v = x * scale
            lo = v - v.astype(o_ref.dtype).astype(F32)
            lane = lax.broadcasted_iota(jnp.int32, x.shape, 1)
            return jnp.where(lane < IDX_HEADS, v, lo)
        else:
            raise ValueError(op)
        return y if scale == 1.0 else y * scale

    def tile(kind):
        src = u_ref if norm else a_ref
        part = src.shape[0] // row_parts
        for r in range(row_parts):
            rows = slice(r * part, (r + 1) * part)
            acc = jnp.dot(src[rows, :], w_ref[...], preferred_element_type=F32)
            if kind == "rms":
                o_ref[rows, :] = _rms(acc, rg_ref[...]).astype(o_ref.dtype)
                continue
            for c, op in enumerate(kind):
                sl = slice(c * LANES, (c + 1) * LANES)
                o_ref[rows, sl] = chunk_epilogue(op, acc[:, sl], rows).astype(o_ref.dtype)

    if len(kinds) == 1:
        tile(kinds[0][0])
    else:
        for kind, ranges in kinds:
            cond = None
            for lo, hi in ranges:
                c = (j >= lo) & (j < hi)
                cond = c if cond is None else (cond | c)
            pl.when(cond)(functools.partial(tile, kind))


def _project(a, w, *, a_col_block=0, gain=None, rms_gain=None, rope_tabs=(), rope_halves=(), kinds,
             seq_len, tm, tn=COL_TILE, out_dtype=BF16):
    n_rows = a.shape[0]
    k_dim, n_cols = w.shape
    norm = gain is not None
    has_rms = rms_gain is not None
    t_blocks = seq_len // tm
    in_specs = [pl.BlockSpec((tm, k_dim), lambda i, j: (i, a_col_block))]
    args = [a]
    if norm:
        in_specs.append(pl.BlockSpec((1, k_dim), lambda i, j: (0, 0)))
        args.append(gain.reshape(1, k_dim).astype(F32))
    in_specs.append(pl.BlockSpec((k_dim, tn), lambda i, j: (0, j)))
    args.append(w)
    if has_rms:
        in_specs.append(pl.BlockSpec((1, tn), lambda i, j: (0, j)))
        args.append(rms_gain)
    for tab in rope_tabs:
        for t in tab:
            in_specs.append(pl.BlockSpec((tm, LANES), lambda i, j: (i % t_blocks, 0)))
            args.append(t)
    scratch = [pltpu.VMEM((tm, k_dim), BF16)] if norm else []
    return pl.pallas_call(
        functools.partial(_proj_kernel, norm=norm, has_rms=has_rms, rope_halves=tuple(rope_halves),
                          kinds=kinds, row_parts=max(1, tm // PROJ_PART_ROWS)),
        grid=(n_rows // tm, n_cols // tn),
        in_specs=in_specs,
        out_specs=pl.BlockSpec((tm, tn), lambda i, j: (i, j)),
        out_shape=jax.ShapeDtypeStruct((n_rows, n_cols), out_dtype),
        scratch_shapes=scratch,
        compiler_params=_cparams("parallel", "arbitrary"),
    )(*args)


def _merge_kernel(oa_ref, ob_ref, ga_ref, gb_ref, mc_ref, x_ref, wa_ref, wb_ref, wo_ref, o_ref):
    ya = jnp.dot(oa_ref[...], wa_ref[...], preferred_element_type=F32)
    merged = mc_ref[...] + ga_ref[...].astype(F32) * ya
    yb = jnp.dot(ob_ref[...], wb_ref[...], preferred_element_type=F32)
    merged = merged + gb_ref[...].astype(F32) * yb
    o_ref[...] = x_ref[...] + jnp.dot(merged.astype(BF16), wo_ref[...], preferred_element_type=F32)


def _merge_project(o_mla, o_dsa, big, gated_conv, x, w_a, w_b, w_out, *, layer, gate_col_block, tm):
    n_rows, d_model = x.shape
    row = lambda col: (lambda i: (i, col))
    resident = lambda w: pl.BlockSpec((None,) + w.shape[1:], lambda i: (layer, 0, 0),
                                      pipeline_mode=pl.Buffered(1))
    return pl.pallas_call(
        _merge_kernel,
        grid=(n_rows // tm,),
        in_specs=[
            pl.BlockSpec((tm, o_mla.shape[1]), row(0)),
            pl.BlockSpec((tm, o_dsa.shape[1]), row(0)),
            pl.BlockSpec((tm, d_model), row(gate_col_block)),
            pl.BlockSpec((tm, d_model), row(gate_col_block + 1)),
            pl.BlockSpec((tm, d_model), row(0)),
            pl.BlockSpec((tm, d_model), row(0)),
            resident(w_a), resident(w_b), resident(w_out),
        ],
        out_specs=pl.BlockSpec((tm, d_model), row(0)),
        out_shape=jax.ShapeDtypeStruct((n_rows, d_model), F32),
        compiler_params=_cparams("parallel"),
    )(o_mla, o_dsa, big, big, gated_conv, x, w_a, w_b, w_out)


def _store_transposed_values(vt_ref, idx, v):
    r = lax.broadcasted_iota(jnp.int32, (HEAD_V, HEAD_V), 0)
    c = lax.broadcasted_iota(jnp.int32, (HEAD_V, HEAD_V), 1)
    eye = jnp.where(r == c, 1.0, 0.0).astype(BF16)
    vt_ref[idx, :HEAD_V, :] = lax.dot_general(eye, v, _NT, preferred_element_type=F32).astype(BF16)
    vt_ref[idx, HEAD_V:, :] = jnp.ones((VT_ROWS - HEAD_V, v.shape[0]), BF16)


def _softmax_step(s, m, acc, vt):
    m_new = jnp.maximum(m, jnp.max(s, axis=0, keepdims=True))
    alpha = jnp.exp2(m - m_new)
    p = jnp.exp2(s - m_new).astype(BF16)
    return m_new, alpha * acc + jnp.dot(vt, p, preferred_element_type=F32)


def _softmax_init(tq):
    return jnp.full((1, tq), MASK_VALUE, F32), jnp.zeros((VT_ROWS, tq), F32)


def _softmax_result(acc):
    return jnp.transpose(acc[:HEAD_V] / acc[HEAD_V:HEAD_V + 1])


def _pipeline_buffers(tk, width):
    return [pltpu.VMEM((tk, width), F32)] * 2 + [pltpu.VMEM((tk, width), BF16)] * 2


def _softmax_pairs(n_pairs, qk, bias, vt_at, bufs, width, tail_masks=None):
    s_a, s_b, p_a, p_b = bufs
    n_blocks = 2 * n_pairs + (2 if tail_masks else 0)
    last = jnp.maximum(n_blocks - 1, 0)
    s_a[...] = qk(0)
    p_b[...] = jnp.zeros(p_b.shape, BF16)

    def stage(c, carry, s_cur, s_next, p_cur, p_prev, mask=None):
        m, alpha_prev, acc = carry
        s_next[...] = qk(jnp.minimum(c + 1, last))
        acc = alpha_prev * acc + jnp.dot(vt_at(jnp.maximum(c - 1, 0)), p_prev[...], preferred_element_type=F32)
        s = s_cur[...] if bias is None else s_cur[...] + bias(c)
        if mask is not None:
            s = mask(s)
        m_new = jnp.maximum(m, jnp.max(s, axis=0, keepdims=True))
        p_cur[...] = jnp.exp2(s - m_new).astype(BF16)
        return m_new, jnp.exp2(m - m_new), acc

    def pair(k, carry, masks=(None, None)):
        carry = stage(2 * k, carry, s_a, s_b, p_a, p_b, masks[0])
        return stage(2 * k + 1, carry, s_b, s_a, p_b, p_a, masks[1])

    m, acc = _softmax_init(width)
    carry = lax.fori_loop(0, n_pairs, pair, (m, jnp.ones((1, width), F32), acc))
    if tail_masks:
        carry = pair(n_pairs, carry, tail_masks)
    m, alpha_prev, acc = carry
    return m, alpha_prev * acc + jnp.dot(vt_at(last), p_b[...], preferred_element_type=F32)


def _mla_flash_kernel(q_ref, kn_ref, v_ref, kr_ref, o_ref, kcat_ref, vt_ref, *bufs, tq, tk):
    qi = pl.program_id(2)

    @pl.when(qi == 0)
    def _():
        kcat_ref[:, :LANES] = kn_ref[...]
        kcat_ref[:, LANES:] = kr_ref[...]
        for c in range(vt_ref.shape[0]):
            _store_transposed_values(vt_ref, c, v_ref[c * tk:(c + 1) * tk, :])

    q = q_ref[...]

    def scores(j):
        off = pl.multiple_of(j * tk, tk)
        return lax.dot_general(kcat_ref[pl.ds(off, tk), :], q, _NT, preferred_element_type=F32)

    def causal(d):
        def mask(s):
            key_pos = d * tk + lax.broadcasted_iota(jnp.int32, s.shape, 0)
            query_pos = lax.broadcasted_iota(jnp.int32, s.shape, 1)
            return jnp.where(key_pos <= query_pos, s, MASK_VALUE)
        return mask

    _, acc = _softmax_pairs(qi, scores, None, lambda j: vt_ref[j], bufs, tq, tail_masks=(causal(0), causal(1)))
    o_ref[...] = _softmax_result(acc).astype(o_ref.dtype)


def _mla_attention(q, kv, big, *, batch, seq_len, kr_col_block, tq, tk):
    assert tq == 2 * tk
    n_rows = batch * seq_len
    nq = seq_len // tq
    return pl.pallas_call(
        functools.partial(_mla_flash_kernel, tq=tq, tk=tk),
        grid=(batch, MLA_HEADS, nq),
        in_specs=[
            pl.BlockSpec((tq, 2 * LANES), lambda b, h, i: (b * nq + i, h)),
            pl.BlockSpec((seq_len, LANES), lambda b, h, i: (b, h)),
            pl.BlockSpec((seq_len, LANES), lambda b, h, i: (b, MLA_HEADS + h)),
            pl.BlockSpec((seq_len, LANES), lambda b, h, i: (b, kr_col_block)),
        ],
        out_specs=pl.BlockSpec((tq, MLA_V), lambda b, h, i: (b * nq + i, h)),
        out_shape=jax.ShapeDtypeStruct((n_rows, MLA_HEADS * MLA_V), BF16),
        scratch_shapes=[pltpu.VMEM((seq_len, 2 * LANES), BF16), pltpu.VMEM((seq_len // tk, VT_ROWS, tk), BF16),
                        *_pipeline_buffers(tk, tq)],
        compiler_params=_cparams("arbitrary", "arbitrary", "arbitrary"),
    )(q, kv, kv, big)


def _dsa_kernel(qa_ref, qi_ref, w_ref, ka_ref, va_ref, ke_ref, ko_ref, o_ref,
                key_ref, bias_ref, wt_ref, vt_ref, qs_ref, kg_ref, oacc_ref, *bufs, tq, tk, top_k, seq_bits):
    i = pl.program_id(1)
    n_chunks = (i * tq + tq + tk - 1) // tk
    n_groups = tk // SUBLANES
    query_pos = i * tq + lax.broadcasted_iota(jnp.int32, (tk, tq), 1)
    key_in_chunk = lax.broadcasted_iota(jnp.int32, (tk, tq), 0)
    key_in_group = lax.broadcasted_iota(jnp.int32, (SUBLANES, tq), 0)

    @pl.when(i == 0)
    def _():
        for g in range(DSA_KV_HEADS):
            kg_ref[g] = ka_ref[:, g * DSA_HEAD_DIM:(g + 1) * DSA_HEAD_DIM]
            for c in range(key_ref.shape[0]):
                v = va_ref[c * tk:(c + 1) * tk, g * DSA_HEAD_DIM:(g + 1) * DSA_HEAD_DIM]
                _store_transposed_values(vt_ref, g * key_ref.shape[0] + c, v)

    wf = w_ref[...].astype(F32)
    wt_ref[...] = jnp.transpose(wf + pltpu.roll(wf, LANES - IDX_HEADS, 1))

    grp = DSA_HEADS // DSA_KV_HEADS
    for h in range(DSA_HEADS):
        qs_ref[h // grp, (h % grp) * tq:(h % grp + 1) * tq, :] = qa_ref[:, h * DSA_HEAD_DIM:(h + 1) * DSA_HEAD_DIM]

    def score_chunk(c, _):
        off = pl.multiple_of(c * tk, tk)
        ke = ke_ref[pl.ds(off, tk), :]
        ko = ko_ref[pl.ds(off, tk), :]
        sc = jnp.zeros((tk, tq), F32)
        for ch in range(IDX_HEADS // 2):
            qc = qi_ref[:, ch * LANES:(ch + 1) * LANES]
            de = lax.dot_general(ke, qc, _NT, preferred_element_type=F32)
            do = lax.dot_general(ko, qc, _NT, preferred_element_type=F32)
            sc = (sc + wt_ref[2 * ch:2 * ch + 1, :] * jnp.maximum(de, 0.0)
                  + wt_ref[2 * ch + 1:2 * ch + 2, :] * jnp.maximum(do, 0.0))
        sc = jnp.where(sc == 0.0, 0.0, sc)
        bits = pltpu.bitcast(sc, jnp.int32)
        key = jnp.where(bits < 0, bits ^ jnp.int32(0x7FFFFFFF), bits)
        key_ref[c] = jnp.where(off + key_in_chunk <= query_pos, key, jnp.int32(INT_MIN))
        return 0

    lax.fori_loop(0, n_chunks, score_chunk, 0)

    def count(pred):
        def body(c, parts):
            parts = list(parts)
            for r in range(n_groups):
                key = key_ref[c, r * SUBLANES:(r + 1) * SUBLANES, :]
                hit = jnp.where(pred(key, c * tk + r * SUBLANES + key_in_group), 1, 0)
                parts[r % len(parts)] = parts[r % len(parts)] + hit
            return tuple(parts)
        parts = lax.fori_loop(0, n_chunks, body, (jnp.zeros((SUBLANES, tq), jnp.int32),) * 4)
        return jnp.sum(parts[0] + parts[1] + parts[2] + parts[3], axis=0, keepdims=True)

    zero = jnp.zeros((1, tq), jnp.int32)
    thr = jnp.where(count(lambda key, col: key >= zero) >= top_k, zero, jnp.int32(INT_MIN))

    def search(t, thr):
        cand = thr | (jnp.int32(1) << (30 - t))
        return jnp.where(count(lambda key, col: key >= cand) >= top_k, cand, thr)

    thr = lax.fori_loop(0, 31, search, thr)
    thr = jnp.maximum(thr, jnp.int32(INT_MIN + 1))

    need = top_k - count(lambda key, col: key > thr)
    n_eq = count(lambda key, col: key == thr)
    has_excess = jnp.max(jnp.where(n_eq > need, 1, 0)) > 0

    def tie_cut():
        def step(t, cut):
            cand = cut | (jnp.int32(1) << (seq_bits - 1 - t))
            cnt = count(lambda key, col: (key == thr) & (col < cand))
            return jnp.where(cnt < need, cand, cut)
        return lax.fori_loop(0, seq_bits, step, zero)

    cut = lax.cond(has_excess, tie_cut, lambda: jnp.full((1, tq), 2 ** seq_bits, jnp.int32))

    def bias_chunk(c, _):
        key = key_ref[c]
        sel = (key > thr) | ((key == thr) & (c * tk + key_in_chunk <= cut))
        bias_ref[c] = jnp.where(sel, 0.0, MASK_VALUE)
        return 0

    lax.fori_loop(0, n_chunks, bias_chunk, 0)

    s_a, s_b, p_a, p_b = bufs
    width = grp * tq
    n_ck = key_ref.shape[0]

    def scores(g, c):
        off = pl.multiple_of(c * tk, tk)
        return lax.dot_general(kg_ref[g, pl.ds(off, tk), :], qs_ref[g], _NT, preferred_element_type=F32)

    def advance(g, c):
        wrap = c + 1 >= n_chunks
        g1, c1 = jnp.where(wrap, g + 1, g), jnp.where(wrap, 0, c + 1)
        done = g1 >= DSA_KV_HEADS
        return jnp.where(done, g, g1), jnp.where(done, c, c1)

    def stage(carry, s_cur, s_next, p_cur, p_prev):
        m, alpha_prev, acc, g, c, g_prev, c_prev = carry
        g_next, c_next = advance(g, c)
        s_next[...] = scores(g_next, c_next)
        acc = alpha_prev * acc + jnp.dot(vt_ref[g_prev * n_ck + c_prev], p_prev[...], preferred_element_type=F32)
        oacc_ref[g_prev] = acc
        m = jnp.where(c == 0, MASK_VALUE, m)
        s = s_cur[...] + jnp.concatenate([bias_ref[c]] * grp, axis=1)
        m_new = jnp.maximum(m, jnp.max(s, axis=0, keepdims=True))
        p_cur[...] = jnp.exp2(s - m_new).astype(BF16)
        return m_new, jnp.exp2(m - m_new), acc, g_next, c_next, g, c

    def pair(_, carry):
        return stage(stage(carry, s_a, s_b, p_a, p_b), s_b, s_a, p_b, p_a)

    s_a[...] = scores(0, 0)
    p_b[...] = jnp.zeros(p_b.shape, BF16)
    m0, acc0 = _softmax_init(width)
    zero = jnp.int32(0)
    carry = lax.fori_loop(0, DSA_KV_HEADS // 2 * n_chunks, pair,
                          (m0, jnp.ones((1, width), F32), acc0, zero, zero, zero, zero))
    _, alpha_prev, acc, _, _, g_prev, c_prev = carry
    oacc_ref[DSA_KV_HEADS - 1] = alpha_prev * acc + jnp.dot(vt_ref[g_prev * n_ck + c_prev], p_b[...],
                                                           preferred_element_type=F32)
    for g in range(DSA_KV_HEADS):
        out = oacc_ref[g, :HEAD_V, :] / oacc_ref[g, HEAD_V:HEAD_V + 1, :]
        for r in range(grp):
            h = g * grp + r
            o_ref[:, h * DSA_HEAD_DIM:(h + 1) * DSA_HEAD_DIM] = jnp.transpose(
                out[:, r * tq:(r + 1) * tq]).astype(o_ref.dtype)


def _dsa_attention(big, *, batch, seq_len, off, tq, tk):
    n_rows = batch * seq_len
    nq = seq_len // tq
    top_k = min(INDEX_TOPK, seq_len // 4)
    seq_bits = int(seq_len).bit_length() - 1
    assert 2 ** seq_bits == seq_len and tk >= top_k
    qa_w, qi_w, kv_w = DSA_HEADS * DSA_HEAD_DIM, IDX_HEADS * IDX_HEAD_DIM, DSA_KV_HEADS * DSA_HEAD_DIM
    small = off["small"] // LANES
    rowmap = lambda col: (lambda b, i: (b * nq + i, col))
    seqmap = lambda col: (lambda b, i: (b, col))
    n_ck = seq_len // tk
    return pl.pallas_call(
        functools.partial(_dsa_kernel, tq=tq, tk=tk, top_k=top_k, seq_bits=seq_bits),
        grid=(batch, nq),
        in_specs=[
            pl.BlockSpec((tq, qa_w), rowmap(off["q_a"] // qa_w)),
            pl.BlockSpec((tq, qi_w), rowmap(off["q_i"] // qi_w)),
            pl.BlockSpec((tq, LANES), rowmap(small + 3)),
            pl.BlockSpec((seq_len, kv_w), seqmap(off["k_a"] // kv_w), pipeline_mode=pl.Buffered(1)),
            pl.BlockSpec((seq_len, kv_w), seqmap(off["v_a"] // kv_w), pipeline_mode=pl.Buffered(1)),
            pl.BlockSpec((seq_len, LANES), seqmap(small + 1), pipeline_mode=pl.Buffered(1)),
            pl.BlockSpec((seq_len, LANES), seqmap(small + 2), pipeline_mode=pl.Buffered(1)),
        ],
        out_specs=pl.BlockSpec((tq, qa_w), lambda b, i: (b * nq + i, 0)),
        out_shape=jax.ShapeDtypeStruct((n_rows, qa_w), BF16),
        scratch_shapes=[pltpu.VMEM((n_ck, tk, tq), jnp.int32), pltpu.VMEM((n_ck, tk, tq), F32),
                        pltpu.VMEM((LANES, tq), F32), pltpu.VMEM((DSA_KV_HEADS * n_ck, VT_ROWS, tk), BF16),
                        pltpu.VMEM((DSA_KV_HEADS, DSA_HEADS // DSA_KV_HEADS * tq, DSA_HEAD_DIM), BF16),
                        pltpu.VMEM((DSA_KV_HEADS, seq_len, DSA_HEAD_DIM), BF16),
                        pltpu.VMEM((DSA_KV_HEADS, VT_ROWS, DSA_HEADS // DSA_KV_HEADS * tq), F32),
                        *_pipeline_buffers(tk, DSA_HEADS // DSA_KV_HEADS * tq)],
        compiler_params=_cparams("arbitrary", "arbitrary"),
    )(big, big, big, big, big, big, big)


def _conv_kernel(a_ref, g_ref, ha_ref, hg_ref, gate_ref, wdw_ref, bdw_ref, lng_ref, lnb_ref, wo_ref,
                 o_ref, y_ref, c_ref, z_ref, *, tt, rows, cols):
    t = pl.program_id(1)
    y_ref[CONV_HALO:, :] = a_ref[...].astype(F32) * _sigmoid(g_ref[...].astype(F32))
    halo = ha_ref[...].astype(F32) * _sigmoid(hg_ref[...].astype(F32))
    y_ref[:CONV_HALO, :] = jnp.where(t == 0, 0.0, halo)

    n_ch = y_ref.shape[1]
    lead = CONV_HALO - (CONV_WIDTH - 1)
    taps_of = [[j for j in range(CONV_WIDTH) if j % SUBLANES == b] for b in range(SUBLANES)]
    for c0 in range(0, n_ch, cols):
        cs = slice(c0, c0 + cols)
        for b, taps in enumerate(taps_of):
            n = tt + taps[-1] - b
            z_ref[b, :n, :] = y_ref[lead + b:lead + b + n, cs]

        def row_group(i, _, cs=cs):
            r0 = pl.multiple_of(i * rows, rows)
            acc = jnp.broadcast_to(bdw_ref[:, cs], (rows, cols))
            for b, taps in enumerate(taps_of):
                for j in taps:
                    acc = acc + wdw_ref[j:j + 1, cs] * z_ref[b, pl.ds(r0 + j - b, rows), :]
            c_ref[pl.ds(r0, rows), cs] = acc
            return 0

        lax.fori_loop(0, tt // rows, row_group, 0)

    x = c_ref[...]
    mu = jnp.mean(x, axis=-1, keepdims=True)
    xc = x - mu
    var = jnp.mean(xc * xc, axis=-1, keepdims=True)
    yn = xc * lax.rsqrt(var + NORM_EPS) * lng_ref[...] + lnb_ref[...]
    z = yn * _sigmoid(yn)
    proj = jnp.dot(z.astype(BF16), wo_ref[...], preferred_element_type=F32)
    o_ref[...] = (gate_ref[...].astype(F32) * proj).astype(o_ref.dtype)


def _conv_branch(big, w_dw, b_dw, ln_g, ln_b, w_o, *, layer, batch, seq_len, off, d_model, tt):
    n_rows = batch * seq_len
    n_ch = w_dw.shape[1]
    nt = seq_len // tt
    hb = tt // CONV_HALO
    w_pad = jnp.concatenate([w_dw.astype(F32), jnp.zeros((CONV_HALO - CONV_WIDTH, n_ch), F32)], axis=0)
    ca, cg, gc = off["conv_a"] // n_ch, off["conv_g"] // n_ch, off["gate"] // d_model + 2
    cur = lambda col: (lambda b, t: (b * nt + t, col))
    halo = lambda col: (lambda b, t: (jnp.maximum((b * nt + t) * hb - 1, 0), col))
    const = lambda b, t: (0, 0)
    vec = lambda v: v.reshape(1, n_ch).astype(F32)
    return pl.pallas_call(
        functools.partial(_conv_kernel, tt=tt, rows=CONV_ROWS, cols=min(CONV_COLS, n_ch)),
        grid=(batch, nt),
        in_specs=[
            pl.BlockSpec((tt, n_ch), cur(ca)),
            pl.BlockSpec((tt, n_ch), cur(cg)),
            pl.BlockSpec((CONV_HALO, n_ch), halo(ca)),
            pl.BlockSpec((CONV_HALO, n_ch), halo(cg)),
            pl.BlockSpec((tt, d_model), cur(gc)),
            pl.BlockSpec((CONV_HALO, n_ch), const),
            pl.BlockSpec((1, n_ch), const),
            pl.BlockSpec((1, n_ch), const),
            pl.BlockSpec((1, n_ch), const),
            pl.BlockSpec((None, n_ch, d_model), lambda b, t: (layer, 0, 0), pipeline_mode=pl.Buffered(1)),
        ],
        out_specs=pl.BlockSpec((tt, d_model), lambda b, t: (b * nt + t, 0)),
        out_shape=jax.ShapeDtypeStruct((n_rows, d_model), F32),
        scratch_shapes=[pltpu.VMEM((tt + CONV_HALO, n_ch), F32), pltpu.VMEM((tt, n_ch), F32),
                        pltpu.VMEM((SUBLANES, tt + CONV_HALO - SUBLANES, min(CONV_COLS, n_ch)), F32)],
        compiler_params=_cparams("parallel", "arbitrary"),
    )(big, big, big, big, big, w_pad, vec(b_dw), vec(ln_g), vec(ln_b), w_o)


def _mlp_kernel(x_ref, g_ref, wu_ref, wd_ref, gf_ref, o_ref, u_ref, acc_ref, *, final_norm):
    f = pl.program_id(1)

    @pl.when(f == 0)
    def _():
        u_ref[...] = _rms(x_ref[...], g_ref[...]).astype(BF16)
        acc_ref[...] = jnp.zeros_like(acc_ref)

    h = jnp.maximum(jnp.dot(u_ref[...], wu_ref[...], preferred_element_type=F32), 0.0)
    acc_ref[...] += jnp.dot((h * h).astype(BF16), wd_ref[...], preferred_element_type=F32)

    @pl.when(f == pl.num_programs(1) - 1)
    def _():
        y = x_ref[...] + acc_ref[...]
        if final_norm:
            y = _rms(y, gf_ref[...])
        o_ref[...] = y


def _mlp(x, gain, w_up, w_down, final_gain, *, layer, final_norm, tm, tf):
    n_rows, d_model = x.shape
    d_ff = w_up.shape[2]
    vec = lambda v: v.reshape(1, d_model).astype(F32)
    return pl.pallas_call(
        functools.partial(_mlp_kernel, final_norm=final_norm),
        grid=(n_rows // tm, d_ff // tf),
        in_specs=[
            pl.BlockSpec((tm, d_model), lambda i, f: (i, 0)),
            pl.BlockSpec((1, d_model), lambda i, f: (0, 0)),
            pl.BlockSpec((None, d_model, tf), lambda i, f: (layer, 0, f)),
            pl.BlockSpec((None, tf, d_model), lambda i, f: (layer, f, 0)),
            pl.BlockSpec((1, d_model), lambda i, f: (0, 0)),
        ],
        out_specs=pl.BlockSpec((tm, d_model), lambda i, f: (i, 0)),
        out_shape=jax.ShapeDtypeStruct((n_rows, d_model), F32),
        scratch_shapes=[pltpu.VMEM((tm, d_model), BF16), pltpu.VMEM((tm, d_model), F32)],
        compiler_params=_cparams("parallel", "arbitrary"),
    )(x, vec(gain), w_up, w_down, vec(final_gain))


def _prep_w_in(w_in, d_model):
    widths = [MLA_Q_LORA, MLA_KV_LORA, MLA_ROPE, DSA_HEADS * DSA_HEAD_DIM, DSA_KV_HEADS * DSA_HEAD_DIM,
              DSA_KV_HEADS * DSA_HEAD_DIM, IDX_HEADS * IDX_HEAD_DIM, IDX_HEAD_DIM, IDX_HEADS,
              2 * d_model, N_BRANCH * d_model]
    splits = [int(v) for v in np.cumsum(widths)[:-1]]
    c_q, c_kv, k_rope, q_a, k_a, v_a, q_i, k_i, w_i, conv_in, gates = jnp.split(w_in, splits, axis=1)
    z = lambda n: jnp.zeros((w_in.shape[0], n), w_in.dtype)
    small = jnp.concatenate([k_rope, z(64), k_i, z(64), z(64), k_i, w_i, w_i, z(LANES - 2 * IDX_HEADS)], axis=1)
    return jnp.concatenate([q_a, q_i, c_q, c_kv, conv_in, gates, k_a, v_a, small], axis=1).astype(BF16)


def _prep_w_uq(w_uq):
    k = w_uq.shape[0]
    w = w_uq.reshape(k, MLA_HEADS, MLA_NOPE + MLA_ROPE)
    w = jnp.concatenate([w, jnp.zeros((k, MLA_HEADS, 2 * LANES - MLA_NOPE - MLA_ROPE), w.dtype)], axis=2)
    return w.reshape(k, MLA_HEADS * 2 * LANES).astype(BF16)


def _prep_w_ukv(w_ukv):
    k = w_ukv.shape[0]
    w = w_ukv.reshape(k, MLA_HEADS, 2, MLA_NOPE)
    return jnp.transpose(w, (0, 2, 1, 3)).reshape(k, 2 * MLA_HEADS * MLA_NOPE).astype(BF16)


def _tile_ranges(off):
    t = lambda name: off[name] // COL_TILE
    per_tile = COL_TILE // LANES
    dsa_scale = float(DSA_HEAD_DIM ** -0.5 * LOG2_E)
    idx_scale = float((IDX_HEADS * IDX_HEAD_DIM) ** -0.5)
    rope_q = (("rope", 1, dsa_scale),) * per_tile
    rope_k = (("rope", 1, 1.0),) * per_tile
    rope_idx = (("rope", 2, 1.0),) * per_tile
    plain = (("plain", None, 1.0),) * per_tile
    sigm = (("sigmoid", None, 1.0),) * per_tile
    small = (("rope", 0, 1.0), ("rope", 2, 1.0), ("rope", 2, 1.0), ("hilo", None, idx_scale))
    return (
        (rope_q, ((t("q_a"), t("q_i")),)),
        (rope_k, ((t("k_a"), t("v_a")),)),
        (rope_idx, ((t("q_i"), t("c_q")),)),
        ("rms", ((t("c_q"), t("conv_a")),)),
        (plain, ((t("conv_a"), t("gate")), (t("v_a"), t("small")))),
        (sigm, ((t("gate"), t("k_a")),)),
        (small, ((t("small"), t("total")),)),
    )


def _pick(n, pref):
    return pref if n % pref == 0 else n


def kernel(x, attn_norm, w_in, mla_q_norm, mla_kv_norm, mla_w_uq, mla_w_ukv, conv_w_dw, conv_b_dw,
           conv_ln_g, conv_ln_b, w_o_mla, w_o_dsa, w_o_conv, w_out, mlp_norm, w_up, w_down, final_norm):
    batch, seq_len, d_model = x.shape
    depth = w_in.shape[0]
    n_rows = batch * seq_len
    off = _in_layout(d_model)
    kinds = _tile_ranges(off)

    tab_mla, half_mla = _rope_lane_tables(seq_len, MLA_ROPE, (0,))
    tab_dsa, half_dsa = _rope_lane_tables(seq_len, DSA_ROT, (0,))
    tab_idx, half_idx = _rope_lane_tables(seq_len, IDX_ROT, (0, IDX_HEAD_DIM))
    tabs = (tab_mla, tab_dsa, tab_idx)
    halves = (half_mla, half_dsa, half_idx)

    mla_scale = float((MLA_NOPE + MLA_ROPE) ** -0.5 * LOG2_E)
    mla_cols = MLA_HEADS * 2 * LANES
    q_kind = (("plain", None, mla_scale), ("rope", 0, mla_scale)) * MLA_HEADS
    kv_kind = (("plain", None, 1.0),) * (mla_cols // LANES)
    everywhere = ((0, 1),)

    w_o_mla_b, w_o_dsa_b, w_o_conv_b, w_out_b, w_up_b, w_down_b = (
        w.astype(BF16) for w in (w_o_mla, w_o_dsa, w_o_conv, w_out, w_up, w_down))

    tm = _pick(seq_len, 512)
    xf = x.reshape(n_rows, d_model)
    for i in range(depth):
        ones = jnp.ones((off["total"],), F32)
        rms_gain = lax.dynamic_update_slice(ones, mla_q_norm[i].astype(F32), (off["c_q"],))
        rms_gain = lax.dynamic_update_slice(rms_gain, mla_kv_norm[i].astype(F32), (off["c_kv"],))
        big = _project(xf, _prep_w_in(w_in[i], d_model), gain=attn_norm[i], rms_gain=rms_gain.reshape(1, -1),
                       rope_tabs=tabs, rope_halves=halves, kinds=kinds, seq_len=seq_len,
                       tm=_pick(seq_len, 1024))

        q_mla = _project(big, _prep_w_uq(mla_w_uq[i]), a_col_block=off["c_q"] // MLA_Q_LORA,
                         rope_tabs=(tab_mla,), rope_halves=(half_mla,), kinds=((q_kind, everywhere),),
                         seq_len=seq_len, tm=tm, tn=mla_cols)
        kv_mla = _project(big, _prep_w_ukv(mla_w_ukv[i]), a_col_block=off["c_kv"] // MLA_KV_LORA,
                          kinds=((kv_kind, everywhere),), seq_len=seq_len, tm=tm, tn=mla_cols)
        o_mla = _mla_attention(q_mla, kv_mla, big, batch=batch, seq_len=seq_len,
                               kr_col_block=off["small"] // LANES, tq=_pick(seq_len, 1024), tk=512)
        o_dsa = _dsa_attention(big, batch=batch, seq_len=seq_len, off=off, tq=_pick(seq_len, 256), tk=512)

        gated_conv = _conv_branch(big, conv_w_dw[i], conv_b_dw[i], conv_ln_g[i], conv_ln_b[i], w_o_conv_b,
                                  layer=i, batch=batch, seq_len=seq_len, off=off, d_model=d_model,
                                  tt=_pick(seq_len, 256))
        xf = _merge_project(o_mla, o_dsa, big, gated_conv, xf, w_o_mla_b, w_o_dsa_b, w_out_b, layer=i,
                            gate_col_block=off["gate"] // d_model, tm=_pick(seq_len, 256))
        xf = _mlp(xf, mlp_norm[i], w_up_b, w_down_b, final_norm, layer=i,
                  final_norm=(i == depth - 1), tm=tm, tf=512)
    return xf.reshape(batch, seq_len, d_model)
```

```python
import functools

import jax
import jax.numpy as jnp
import numpy as np
from jax import lax
from jax.experimental import pallas as pl
from jax.experimental.pallas import tpu as pltpu

F32 = jnp.float32
BF16 = jnp.bfloat16

ROPE_THETA = 500000.0
NORM_EPS = 1e-6
N_BRANCH = 3

MLA_HEADS = 16
MLA_Q_LORA = 512
MLA_KV_LORA = 512
MLA_NOPE = 128
MLA_ROPE = 64
MLA_V = 128

DSA_HEADS = 16
DSA_KV_HEADS = 4
DSA_HEAD_DIM = 128
DSA_ROT = DSA_HEAD_DIM // 4
IDX_HEADS = 16
IDX_HEAD_DIM = 64
IDX_ROT = IDX_HEAD_DIM // 4
INDEX_TOPK = 256

CONV_WIDTH = 31

LANES = 128
SUBLANES = 8
COL_TILE = 512
PROJ_PART_ROWS = 256
CONV_HALO = 32
CONV_ROWS, CONV_COLS = 32, 512
MASK_VALUE = -1e30
LOG2_E = 1.4426950408889634
HEAD_V = 128
VT_ROWS = HEAD_V + 16
INT_MIN = -2 ** 31
VMEM_LIMIT = 56 * 1024 * 1024

_NT = (((1,), (1,)), ((), ()))


def _cparams(*sem):
    return pltpu.CompilerParams(dimension_semantics=sem, vmem_limit_bytes=VMEM_LIMIT)


def _in_layout(d_model):
    c = d_model
    off = {"q_a": 0, "q_i": 2048, "c_q": 3072, "c_kv": 3584, "conv_a": 4096, "conv_g": 4096 + c,
           "gate": 4096 + 2 * c}
    off["k_a"] = off["gate"] + N_BRANCH * d_model
    off["v_a"] = off["k_a"] + 512
    off["small"] = off["v_a"] + 512
    off["total"] = off["small"] + 512
    return off


def _rope_lane_tables(seq_len, rot_dim, lane_starts):
    half = rot_dim // 2
    inv = ROPE_THETA ** (-jnp.arange(0, rot_dim, 2, dtype=F32) / rot_dim)
    ang = jnp.arange(seq_len, dtype=F32)[:, None] * inv[None, :]
    cos, sin = jnp.cos(ang), jnp.sin(ang)
    ones = lambda n: jnp.ones((seq_len, n), F32)
    zeros = lambda n: jnp.zeros((seq_len, n), F32)
    c_parts, a_parts, b_parts = [], [], []
    pos = 0
    for s0 in lane_starts:
        gap = s0 - pos
        c_parts += [ones(gap), cos, cos]
        a_parts += [zeros(gap), -sin, zeros(half)]
        b_parts += [zeros(gap), zeros(half), sin]
        pos = s0 + 2 * half
    c_parts.append(ones(LANES - pos))
    a_parts.append(zeros(LANES - pos))
    b_parts.append(zeros(LANES - pos))
    cat = lambda p: jnp.concatenate(p, axis=1)
    return (cat(c_parts), cat(a_parts), cat(b_parts)), half


def _sigmoid(x):
    return 0.5 * jnp.tanh(0.5 * x) + 0.5


def _rms(x, gain):
    ms = jnp.mean(x * x, axis=-1, keepdims=True)
    return x * lax.rsqrt(ms + NORM_EPS) * gain


def _proj_kernel(*refs, norm, has_rms, rope_halves, kinds, row_parts):
    it = iter(refs)
    a_ref = next(it)
    ga_ref = next(it) if norm else None
    w_ref = next(it)
    rg_ref = next(it) if has_rms else None
    tabs = [(next(it), next(it), next(it)) for _ in rope_halves]
    o_ref = next(it)
    u_ref = next(it) if norm else None
    j = pl.program_id(1)

    if norm:
        @pl.when(j == 0)
        def _():
            u_ref[...] = _rms(a_ref[...], ga_ref[...]).astype(BF16)

    def chunk_epilogue(op, x, rows):
        name, arg, scale = op
        if name == "plain":
            y = x
        elif name == "sigmoid":
            y = _sigmoid(x)
        elif name == "rope":
            c_ref, sa_ref, sb_ref = tabs[arg]
            half = rope_halves[arg]
            y = (x * c_ref[rows, :] + pltpu.roll(x, LANES - half, 1) * sa_ref[rows, :]
                 + pltpu.roll(x, half, 1) * sb_ref[rows, :])
        elif name == "hilo":
            v = x * scale
            lo = v - v.astype(o_ref.dtype).astype(F32)
            lane = lax.broadcasted_iota(jnp.int32, x.shape, 1)
            return jnp.where(lane < IDX_HEADS, v, lo)
        else:
            raise ValueError(op)
        return y if scale == 1.0 else y * scale

    def tile(kind):
        src = u_ref if norm else a_ref
        part = src.shape[0] // row_parts
        for r in range(row_parts):
            rows = slice(r * part, (r + 1) * part)
            acc = jnp.dot(src[rows, :], w_ref[...], preferred_element_type=F32)
            if kind == "rms":
                o_ref[rows, :] = _rms(acc, rg_ref[...]).astype(o_ref.dtype)
                continue
            for c, op in enumerate(kind):
                sl = slice(c * LANES, (c + 1) * LANES)
                o_ref[rows, sl] = chunk_epilogue(op, acc[:, sl], rows).astype(o_ref.dtype)

    if len(kinds) == 1:
        tile(kinds[0][0])
    else:
        for kind, ranges in kinds:
            cond = None
            for lo, hi in ranges:
                c = (j >= lo) & (j < hi)
                cond = c if cond is None else (cond | c)
            pl.when(cond)(functools.partial(tile, kind))


def _project(a, w, *, a_col_block=0, gain=None, rms_gain=None, rope_tabs=(), rope_halves=(), kinds,
             seq_len, tm, tn=COL_TILE, out_dtype=BF16):
    n_rows = a.shape[0]
    k_dim, n_cols = w.shape
    norm = gain is not None
    has_rms = rms_gain is not None
    t_blocks = seq_len // tm
    in_specs = [pl.BlockSpec((tm, k_dim), lambda i, j: (i, a_col_block))]
    args = [a]
    if norm:
        in_specs.append(pl.BlockSpec((1, k_dim), lambda i, j: (0, 0)))
        args.append(gain.reshape(1, k_dim).astype(F32))
    in_specs.append(pl.BlockSpec((k_dim, tn), lambda i, j: (0, j)))
    args.append(w)
    if has_rms:
        in_specs.append(pl.BlockSpec((1, tn), lambda i, j: (0, j)))
        args.append(rms_gain)
    for tab in rope_tabs:
        for t in tab:
            in_specs.append(pl.BlockSpec((tm, LANES), lambda i, j: (i % t_blocks, 0)))
            args.append(t)
    scratch = [pltpu.VMEM((tm, k_dim), BF16)] if norm else []
    return pl.pallas_call(
        functools.partial(_proj_kernel, norm=norm, has_rms=has_rms, rope_halves=tuple(rope_halves),
                          kinds=kinds, row_parts=max(1, tm // PROJ_PART_ROWS)),
        grid=(n_rows // tm, n_cols // tn),
        in_specs=in_specs,
        out_specs=pl.BlockSpec((tm, tn), lambda i, j: (i, j)),
        out_shape=jax.ShapeDtypeStruct((n_rows, n_cols), out_dtype),
        scratch_shapes=scratch,
        compiler_params=_cparams("parallel", "arbitrary"),
    )(*args)


def _merge_kernel(oa_ref, ob_ref, ga_ref, gb_ref, mc_ref, x_ref, wa_ref, wb_ref, wo_ref, o_ref):
    ya = jnp.dot(oa_ref[...], wa_ref[...], preferred_element_type=F32)
    merged = mc_ref[...] + ga_ref[...].astype(F32) * ya
    yb = jnp.dot(ob_ref[...], wb_ref[...], preferred_element_type=F32)
    merged = merged + gb_ref[...].astype(F32) * yb
    o_ref[...] = x_ref[...] + jnp.dot(merged.astype(BF16), wo_ref[...], preferred_element_type=F32)


def _merge_project(o_mla, o_dsa, big, gated_conv, x, w_a, w_b, w_out, *, layer, gate_col_block, tm):
    n_rows, d_model = x.shape
    row = lambda col: (lambda i: (i, col))
    resident = lambda w: pl.BlockSpec((None,) + w.shape[1:], lambda i: (layer, 0, 0),
                                      pipeline_mode=pl.Buffered(1))
    return pl.pallas_call(
        _merge_kernel,
        grid=(n_rows // tm,),
        in_specs=[
            pl.BlockSpec((tm, o_mla.shape[1]), row(0)),
            pl.BlockSpec((tm, o_dsa.shape[1]), row(0)),
            pl.BlockSpec((tm, d_model), row(gate_col_block)),
            pl.BlockSpec((tm, d_model), row(gate_col_block + 1)),
            pl.BlockSpec((tm, d_model), row(0)),
            pl.BlockSpec((tm, d_model), row(0)),
            resident(w_a), resident(w_b), resident(w_out),
        ],
        out_specs=pl.BlockSpec((tm, d_model), row(0)),
        out_shape=jax.ShapeDtypeStruct((n_rows, d_model), F32),
        compiler_params=_cparams("parallel"),
    )(o_mla, o_dsa, big, big, gated_conv, x, w_a, w_b, w_out)


def _store_transposed_values(vt_ref, idx, v):
    r = lax.broadcasted_iota(jnp.int32, (HEAD_V, HEAD_V), 0)
    c = lax.broadcasted_iota(jnp.int32, (HEAD_V, HEAD_V), 1)
    eye = jnp.where(r == c, 1.0, 0.0).astype(BF16)
    vt_ref[idx, :HEAD_V, :] = lax.dot_general(eye, v, _NT, preferred_element_type=F32).astype(BF16)
    vt_ref[idx, HEAD_V:, :] = jnp.ones((VT_ROWS - HEAD_V, v.shape[0]), BF16)


def _softmax_step(s, m, acc, vt):
    m_new = jnp.maximum(m, jnp.max(s, axis=0, keepdims=True))
    alpha = jnp.exp2(m - m_new)
    p = jnp.exp2(s - m_new).astype(BF16)
    return m_new, alpha * acc + jnp.dot(vt, p, preferred_element_type=F32)


def _softmax_init(tq):
    return jnp.full((1, tq), MASK_VALUE, F32), jnp.zeros((VT_ROWS, tq), F32)


def _softmax_result(acc):
    return jnp.transpose(acc[:HEAD_V] / acc[HEAD_V:HEAD_V + 1])


def _pipeline_buffers(tk, width):
    return [pltpu.VMEM((tk, width), F32)] * 2 + [pltpu.VMEM((tk, width), BF16)] * 2


def _softmax_pairs(n_pairs, qk, bias, vt_at, bufs, width, tail_masks=None):
    s_a, s_b, p_a, p_b = bufs
    n_blocks = 2 * n_pairs + (2 if tail_masks else 0)
    last = jnp.maximum(n_blocks - 1, 0)
    s_a[...] = qk(0)
    p_b[...] = jnp.zeros(p_b.shape, BF16)

    def stage(c, carry, s_cur, s_next, p_cur, p_prev, mask=None):
        m, alpha_prev, acc = carry
        s_next[...] = qk(jnp.minimum(c + 1, last))
        acc = alpha_prev * acc + jnp.dot(vt_at(jnp.maximum(c - 1, 0)), p_prev[...], preferred_element_type=F32)
        s = s_cur[...] if bias is None else s_cur[...] + bias(c)
        if mask is not None:
            s = mask(s)
        m_new = jnp.maximum(m, jnp.max(s, axis=0, keepdims=True))
        p_cur[...] = jnp.exp2(s - m_new).astype(BF16)
        return m_new, jnp.exp2(m - m_new), acc

    def pair(k, carry, masks=(None, None)):
        carry = stage(2 * k, carry, s_a, s_b, p_a, p_b, masks[0])
        return stage(2 * k + 1, carry, s_b, s_a, p_b, p_a, masks[1])

    m, acc = _softmax_init(width)
    carry = lax.fori_loop(0, n_pairs, pair, (m, jnp.ones((1, width), F32), acc))
    if tail_masks:
        carry = pair(n_pairs, carry, tail_masks)
    m, alpha_prev, acc = carry
    return m, alpha_prev * acc + jnp.dot(vt_at(last), p_b[...], preferred_element_type=F32)


def _mla_flash_kernel(q_ref, kn_ref, v_ref, kr_ref, o_ref, kcat_ref, vt_ref, *bufs, tq, tk):
    qi = pl.program_id(2)

    @pl.when(qi == 0)
    def _():
        kcat_ref[:, :LANES] = kn_ref[...]
        kcat_ref[:, LANES:] = kr_ref[...]
        for c in range(vt_ref.shape[0]):
            _store_transposed_values(vt_ref, c, v_ref[c * tk:(c + 1) * tk, :])

    q = q_ref[...]

    def scores(j):
        off = pl.multiple_of(j * tk, tk)
        return lax.dot_general(kcat_ref[pl.ds(off, tk), :], q, _NT, preferred_element_type=F32)

    def causal(d):
        def mask(s):
            key_pos = d * tk + lax.broadcasted_iota(jnp.int32, s.shape, 0)
            query_pos = lax.broadcasted_iota(jnp.int32, s.shape, 1)
            return jnp.where(key_pos <= query_pos, s, MASK_VALUE)
        return mask

    _, acc = _softmax_pairs(qi, scores, None, lambda j: vt_ref[j], bufs, tq, tail_masks=(causal(0), causal(1)))
    o_ref[...] = _softmax_result(acc).astype(o_ref.dtype)


def _mla_attention(q, kv, big, *, batch, seq_len, kr_col_block, tq, tk):
    assert tq == 2 * tk
    n_rows = batch * seq_len
    nq = seq_len // tq
    return pl.pallas_call(
        functools.partial(_mla_flash_kernel, tq=tq, tk=tk),
        grid=(batch, MLA_HEADS, nq),
        in_specs=[
            pl.BlockSpec((tq, 2 * LANES), lambda b, h, i: (b * nq + i, h)),
            pl.BlockSpec((seq_len, LANES), lambda b, h, i: (b, h)),
            pl.BlockSpec((seq_len, LANES), lambda b, h, i: (b, MLA_HEADS + h)),
            pl.BlockSpec((seq_len, LANES), lambda b, h, i: (b, kr_col_block)),
        ],
        out_specs=pl.BlockSpec((tq, MLA_V), lambda b, h, i: (b * nq + i, h)),
        out_shape=jax.ShapeDtypeStruct((n_rows, MLA_HEADS * MLA_V), BF16),
        scratch_shapes=[pltpu.VMEM((seq_len, 2 * LANES), BF16), pltpu.VMEM((seq_len // tk, VT_ROWS, tk), BF16),
                        *_pipeline_buffers(tk, tq)],
        compiler_params=_cparams("arbitrary", "arbitrary", "arbitrary"),
    )(q, kv, kv, big)


def _dsa_kernel(qa_ref, qi_ref, w_ref, ka_ref, va_ref, ke_ref, ko_ref, o_ref,
                key_ref, bias_ref, wt_ref, vt_ref, qs_ref, kg_ref, oacc_ref, *bufs, tq, tk, top_k, seq_bits):
    i = pl.program_id(1)
    n_chunks = (i * tq + tq + tk - 1) // tk
    n_groups = tk // SUBLANES
    query_pos = i * tq + lax.broadcasted_iota(jnp.int32, (tk, tq), 1)
    key_in_chunk = lax.broadcasted_iota(jnp.int32, (tk, tq), 0)
    key_in_group = lax.broadcasted_iota(jnp.int32, (SUBLANES, tq), 0)

    @pl.when(i == 0)
    def _():
        for g in range(DSA_KV_HEADS):
            kg_ref[g] = ka_ref[:, g * DSA_HEAD_DIM:(g + 1) * DSA_HEAD_DIM]
            for c in range(key_ref.shape[0]):
                v = va_ref[c * tk:(c + 1) * tk, g * DSA_HEAD_DIM:(g + 1) * DSA_HEAD_DIM]
                _store_transposed_values(vt_ref, g * key_ref.shape[0] + c, v)

    wf = w_ref[...].astype(F32)
    wt_ref[...] = jnp.transpose(wf + pltpu.roll(wf, LANES - IDX_HEADS, 1))

    grp = DSA_HEADS // DSA_KV_HEADS
    for h in range(DSA_HEADS):
        qs_ref[h // grp, (h % grp) * tq:(h % grp + 1) * tq, :] = qa_ref[:, h * DSA_HEAD_DIM:(h + 1) * DSA_HEAD_DIM]

    def score_chunk(c, _):
        off = pl.multiple_of(c * tk, tk)
        ke = ke_ref[pl.ds(off, tk), :]
        ko = ko_ref[pl.ds(off, tk), :]
        sc = jnp.zeros((tk, tq), F32)
        for ch in range(IDX_HEADS // 2):
            qc = qi_ref[:, ch * LANES:(ch + 1) * LANES]
            de = lax.dot_general(ke, qc, _NT, preferred_element_type=F32)
            do = lax.dot_general(ko, qc, _NT, preferred_element_type=F32)
            sc = (sc + wt_ref[2 * ch:2 * ch + 1, :] * jnp.maximum(de, 0.0)
                  + wt_ref[2 * ch + 1:2 * ch + 2, :] * jnp.maximum(do, 0.0))
        sc = jnp.where(sc == 0.0, 0.0, sc)
        bits = pltpu.bitcast(sc, jnp.int32)
        key = jnp.where(bits < 0, bits ^ jnp.int32(0x7FFFFFFF), bits)
        key_ref[c] = jnp.where(off + key_in_chunk <= query_pos, key, jnp.int32(INT_MIN))
        return 0

    lax.fori_loop(0, n_chunks, score_chunk, 0)

    def count(pred):
        def body(c, parts):
            parts = list(parts)
            for r in range(n_groups):
                key = key_ref[c, r * SUBLANES:(r + 1) * SUBLANES, :]
                hit = jnp.where(pred(key, c * tk + r * SUBLANES + key_in_group), 1, 0)
                parts[r % len(parts)] = parts[r % len(parts)] + hit
            return tuple(parts)
        parts = lax.fori_loop(0, n_chunks, body, (jnp.zeros((SUBLANES, tq), jnp.int32),) * 4)
        return jnp.sum(parts[0] + parts[1] + parts[2] + parts[3], axis=0, keepdims=True)

    zero = jnp.zeros((1, tq), jnp.int32)
    thr = jnp.where(count(lambda key, col: key >= zero) >= top_k, zero, jnp.int32(INT_MIN))

    def search(t, thr):
        cand = thr | (jnp.int32(1) << (30 - t))
        return jnp.where(count(lambda key, col: key >= cand) >= top_k, cand, thr)

    thr = lax.fori_loop(0, 31, search, thr)
    thr = jnp.maximum(thr, jnp.int32(INT_MIN + 1))

    need = top_k - count(lambda key, col: key > thr)
    n_eq = count(lambda key, col: key == thr)
    has_excess = jnp.max(jnp.where(n_eq > need, 1, 0)) > 0

    def tie_cut():
        def step(t, cut):
            cand = cut | (jnp.int32(1) << (seq_bits - 1 - t))
            cnt = count(lambda key, col: (key == thr) & (col < cand))
            return jnp.where(cnt < need, cand, cut)
        return lax.fori_loop(0, seq_bits, step, zero)

    cut = lax.cond(has_excess, tie_cut, lambda: jnp.full((1, tq), 2 ** seq_bits, jnp.int32))

    def bias_chunk(c, _):
        key = key_ref[c]
        sel = (key > thr) | ((key == thr) & (c * tk + key_in_chunk <= cut))
        bias_ref[c] = jnp.where(sel, 0.0, MASK_VALUE)
        return 0

    lax.fori_loop(0, n_chunks, bias_chunk, 0)

    s_a, s_b, p_a, p_b = bufs
    width = grp * tq
    n_ck = key_ref.shape[0]

    def scores(g, c):
        off = pl.multiple_of(c * tk, tk)
        return lax.dot_general(kg_ref[g, pl.ds(off, tk), :], qs_ref[g], _NT, preferred_element_type=F32)

    def advance(g, c):
        wrap = c + 1 >= n_chunks
        g1, c1 = jnp.where(wrap, g + 1, g), jnp.where(wrap, 0, c + 1)
        done = g1 >= DSA_KV_HEADS
        return jnp.where(done, g, g1), jnp.where(done, c, c1)

    def stage(carry, s_cur, s_next, p_cur, p_prev):
        m, alpha_prev, acc, g, c, g_prev, c_prev = carry
        g_next, c_next = advance(g, c)
        s_next[...] = scores(g_next, c_next)
        acc = alpha_prev * acc + jnp.dot(vt_ref[g_prev * n_ck + c_prev], p_prev[...], preferred_element_type=F32)
        oacc_ref[g_prev] = acc
        m = jnp.where(c == 0, MASK_VALUE, m)
        s = s_cur[...] + jnp.concatenate([bias_ref[c]] * grp, axis=1)
        m_new = jnp.maximum(m, jnp.max(s, axis=0, keepdims=True))
        p_cur[...] = jnp.exp2(s - m_new).astype(BF16)
        return m_new, jnp.exp2(m - m_new), acc, g_next, c_next, g, c

    def pair(_, carry):
        return stage(stage(carry, s_a, s_b, p_a, p_b), s_b, s_a, p_b, p_a)

    s_a[...] = scores(0, 0)
    p_b[...] = jnp.zeros(p_b.shape, BF16)
    m0, acc0 = _softmax_init(width)
    zero = jnp.int32(0)
    carry = lax.fori_loop(0, DSA_KV_HEADS // 2 * n_chunks, pair,
                          (m0, jnp.ones((1, width), F32), acc0, zero, zero, zero, zero))
    _, alpha_prev, acc, _, _, g_prev, c_prev = carry
    oacc_ref[DSA_KV_HEADS - 1] = alpha_prev * acc + jnp.dot(vt_ref[g_prev * n_ck + c_prev], p_b[...],
                                                           preferred_element_type=F32)
    for g in range(DSA_KV_HEADS):
        out = oacc_ref[g, :HEAD_V, :] / oacc_ref[g, HEAD_V:HEAD_V + 1, :]
        for r in range(grp):
            h = g * grp + r
            o_ref[:, h * DSA_HEAD_DIM:(h + 1) * DSA_HEAD_DIM] = jnp.transpose(
                out[:, r * tq:(r + 1) * tq]).astype(o_ref.dtype)


def _dsa_attention(big, *, batch, seq_len, off, tq, tk):
    n_rows = batch * seq_len
    nq = seq_len // tq
    top_k = min(INDEX_TOPK, seq_len // 4)
    seq_bits = int(seq_len).bit_length() - 1
    assert 2 ** seq_bits == seq_len and tk >= top_k
    qa_w, qi_w, kv_w = DSA_HEADS * DSA_HEAD_DIM, IDX_HEADS * IDX_HEAD_DIM, DSA_KV_HEADS * DSA_HEAD_DIM
    small = off["small"] // LANES
    rowmap = lambda col: (lambda b, i: (b * nq + i, col))
    seqmap = lambda col: (lambda b, i: (b, col))
    n_ck = seq_len // tk
    return pl.pallas_call(
        functools.partial(_dsa_kernel, tq=tq, tk=tk, top_k=top_k, seq_bits=seq_bits),
        grid=(batch, nq),
        in_specs=[
            pl.BlockSpec((tq, qa_w), rowmap(off["q_a"] // qa_w)),
            pl.BlockSpec((tq, qi_w), rowmap(off["q_i"] // qi_w)),
            pl.BlockSpec((tq, LANES), rowmap(small + 3)),
            pl.BlockSpec((seq_len, kv_w), seqmap(off["k_a"] // kv_w), pipeline_mode=pl.Buffered(1)),
            pl.BlockSpec((seq_len, kv_w), seqmap(off["v_a"] // kv_w), pipeline_mode=pl.Buffered(1)),
            pl.BlockSpec((seq_len, LANES), seqmap(small + 1), pipeline_mode=pl.Buffered(1)),
            pl.BlockSpec((seq_len, LANES), seqmap(small + 2), pipeline_mode=pl.Buffered(1)),
        ],
        out_specs=pl.BlockSpec((tq, qa_w), lambda b, i: (b * nq + i, 0)),
        out_shape=jax.ShapeDtypeStruct((n_rows, qa_w), BF16),
        scratch_shapes=[pltpu.VMEM((n_ck, tk, tq), jnp.int32), pltpu.VMEM((n_ck, tk, tq), F32),
                        pltpu.VMEM((LANES, tq), F32), pltpu.VMEM((DSA_KV_HEADS * n_ck, VT_ROWS, tk), BF16),
                        pltpu.VMEM((DSA_KV_HEADS, DSA_HEADS // DSA_KV_HEADS * tq, DSA_HEAD_DIM), BF16),
                        pltpu.VMEM((DSA_KV_HEADS, seq_len, DSA_HEAD_DIM), BF16),
                        pltpu.VMEM((DSA_KV_HEADS, VT_ROWS, DSA_HEADS // DSA_KV_HEADS * tq), F32),
                        *_pipeline_buffers(tk, DSA_HEADS // DSA_KV_HEADS * tq)],
        compiler_params=_cparams("arbitrary", "arbitrary"),
    )(big, big, big, big, big, big, big)


def _conv_kernel(a_ref, g_ref, ha_ref, hg_ref, gate_ref, wdw_ref, bdw_ref, lng_ref, lnb_ref, wo_ref,
                 o_ref, y_ref, c_ref, z_ref, *, tt, rows, cols):
    t = pl.program_id(1)
    y_ref[CONV_HALO:, :] = a_ref[...].astype(F32) * _sigmoid(g_ref[...].astype(F32))
    halo = ha_ref[...].astype(F32) * _sigmoid(hg_ref[...].astype(F32))
    y_ref[:CONV_HALO, :] = jnp.where(t == 0, 0.0, halo)

    n_ch = y_ref.shape[1]
    lead = CONV_HALO - (CONV_WIDTH - 1)
    taps_of = [[j for j in range(CONV_WIDTH) if j % SUBLANES == b] for b in range(SUBLANES)]
    for c0 in range(0, n_ch, cols):
        cs = slice(c0, c0 + cols)
        for b, taps in enumerate(taps_of):
            n = tt + taps[-1] - b
            z_ref[b, :n, :] = y_ref[lead + b:lead + b + n, cs]

        def row_group(i, _, cs=cs):
            r0 = pl.multiple_of(i * rows, rows)
            acc = jnp.broadcast_to(bdw_ref[:, cs], (rows, cols))
            for b, taps in enumerate(taps_of):
                for j in taps:
                    acc = acc + wdw_ref[j:j + 1, cs] * z_ref[b, pl.ds(r0 + j - b, rows), :]
            c_ref[pl.ds(r0, rows), cs] = acc
            return 0

        lax.fori_loop(0, tt // rows, row_group, 0)

    x = c_ref[...]
    mu = jnp.mean(x, axis=-1, keepdims=True)
    xc = x - mu
    var = jnp.mean(xc * xc, axis=-1, keepdims=True)
    yn = xc * lax.rsqrt(var + NORM_EPS) * lng_ref[...] + lnb_ref[...]
    z = yn * _sigmoid(yn)
    proj = jnp.dot(z.astype(BF16), wo_ref[...], preferred_element_type=F32)
    o_ref[...] = (gate_ref[...].astype(F32) * proj).astype(o_ref.dtype)


def _conv_branch(big, w_dw, b_dw, ln_g, ln_b, w_o, *, layer, batch, seq_len, off, d_model, tt):
    n_rows = batch * seq_len
    n_ch = w_dw.shape[1]
    nt = seq_len // tt
    hb = tt // CONV_HALO
    w_pad = jnp.concatenate([w_dw.astype(F32), jnp.zeros((CONV_HALO - CONV_WIDTH, n_ch), F32)], axis=0)
    ca, cg, gc = off["conv_a"] // n_ch, off["conv_g"] // n_ch, off["gate"] // d_model + 2
    cur = lambda col: (lambda b, t: (b * nt + t, col))
    halo = lambda col: (lambda b, t: (jnp.maximum((b * nt + t) * hb - 1, 0), col))
    const = lambda b, t: (0, 0)
    vec = lambda v: v.reshape(1, n_ch).astype(F32)
    return pl.pallas_call(
        functools.partial(_conv_kernel, tt=tt, rows=CONV_ROWS, cols=min(CONV_COLS, n_ch)),
        grid=(batch, nt),
        in_specs=[
            pl.BlockSpec((tt, n_ch), cur(ca)),
            pl.BlockSpec((tt, n_ch), cur(cg)),
            pl.BlockSpec((CONV_HALO, n_ch), halo(ca)),
            pl.BlockSpec((CONV_HALO, n_ch), halo(cg)),
            pl.BlockSpec((tt, d_model), cur(gc)),
            pl.BlockSpec((CONV_HALO, n_ch), const),
            pl.BlockSpec((1, n_ch), const),
            pl.BlockSpec((1, n_ch), const),
            pl.BlockSpec((1, n_ch), const),
            pl.BlockSpec((None, n_ch, d_model), lambda b, t: (layer, 0, 0), pipeline_mode=pl.Buffered(1)),
        ],
        out_specs=pl.BlockSpec((tt, d_model), lambda b, t: (b * nt + t, 0)),
        out_shape=jax.ShapeDtypeStruct((n_rows, d_model), F32),
        scratch_shapes=[pltpu.VMEM((tt + CONV_HALO, n_ch), F32), pltpu.VMEM((tt, n_ch), F32),
                        pltpu.VMEM((SUBLANES, tt + CONV_HALO - SUBLANES, min(CONV_COLS, n_ch)), F32)],
        compiler_params=_cparams("parallel", "arbitrary"),
    )(big, big, big, big, big, w_pad, vec(b_dw), vec(ln_g), vec(ln_b), w_o)


def _mlp_kernel(x_ref, g_ref, wu_ref, wd_ref, gf_ref, o_ref, u_ref, acc_ref, *, final_norm):
    f = pl.program_id(1)

    @pl.when(f == 0)
    def _():
        u_ref[...] = _rms(x_ref[...], g_ref[...]).astype(BF16)
        acc_ref[...] = jnp.zeros_like(acc_ref)

    h = jnp.maximum(jnp.dot(u_ref[...], wu_ref[...], preferred_element_type=F32), 0.0)
    acc_ref[...] += jnp.dot((h * h).astype(BF16), wd_ref[...], preferred_element_type=F32)

    @pl.when(f == pl.num_programs(1) - 1)
    def _():
        y = x_ref[...] + acc_ref[...]
        if final_norm:
            y = _rms(y, gf_ref[...])
        o_ref[...] = y


def _mlp(x, gain, w_up, w_down, final_gain, *, layer, final_norm, tm, tf):
    n_rows, d_model = x.shape
    d_ff = w_up.shape[2]
    vec = lambda v: v.reshape(1, d_model).astype(F32)
    return pl.pallas_call(
        functools.partial(_mlp_kernel, final_norm=final_norm),
        grid=(n_rows // tm, d_ff // tf),
        in_specs=[
            pl.BlockSpec((tm, d_model), lambda i, f: (i, 0)),
            pl.BlockSpec((1, d_model), lambda i, f: (0, 0)),
            pl.BlockSpec((None, d_model, tf), lambda i, f: (layer, 0, f)),
            pl.BlockSpec((None, tf, d_model), lambda i, f: (layer, f, 0)),
            pl.BlockSpec((1, d_model), lambda i, f: (0, 0)),
        ],
        out_specs=pl.BlockSpec((tm, d_model), lambda i, f: (i, 0)),
        out_shape=jax.ShapeDtypeStruct((n_rows, d_model), F32),
        scratch_shapes=[pltpu.VMEM((tm, d_model), BF16), pltpu.VMEM((tm, d_model), F32)],
        compiler_params=_cparams("parallel", "arbitrary"),
    )(x, vec(gain), w_up, w_down, vec(final_gain))


def _prep_w_in(w_in, d_model):
    widths = [MLA_Q_LORA, MLA_KV_LORA, MLA_ROPE, DSA_HEADS * DSA_HEAD_DIM, DSA_KV_HEADS * DSA_HEAD_DIM,
              DSA_KV_HEADS * DSA_HEAD_DIM, IDX_HEADS * IDX_HEAD_DIM, IDX_HEAD_DIM, IDX_HEADS,
              2 * d_model, N_BRANCH * d_model]
    splits = [int(v) for v in np.cumsum(widths)[:-1]]
    c_q, c_kv, k_rope, q_a, k_a, v_a, q_i, k_i, w_i, conv_in, gates = jnp.split(w_in, splits, axis=1)
    z = lambda n: jnp.zeros((w_in.shape[0], n), w_in.dtype)
    small = jnp.concatenate([k_rope, z(64), k_i, z(64), z(64), k_i, w_i, w_i, z(LANES - 2 * IDX_HEADS)], axis=1)
    return jnp.concatenate([q_a, q_i, c_q, c_kv, conv_in, gates, k_a, v_a, small], axis=1).astype(BF16)


def _prep_w_uq(w_uq):
    k = w_uq.shape[0]
    w = w_uq.reshape(k, MLA_HEADS, MLA_NOPE + MLA_ROPE)
    w = jnp.concatenate([w, jnp.zeros((k, MLA_HEADS, 2 * LANES - MLA_NOPE - MLA_ROPE), w.dtype)], axis=2)
    return w.reshape(k, MLA_HEADS * 2 * LANES).astype(BF16)


def _prep_w_ukv(w_ukv):
    k = w_ukv.shape[0]
    w = w_ukv.reshape(k, MLA_HEADS, 2, MLA_NOPE)
    return jnp.transpose(w, (0, 2, 1, 3)).reshape(k, 2 * MLA_HEADS * MLA_NOPE).astype(BF16)


def _tile_ranges(off):
    t = lambda name: off[name] // COL_TILE
    per_tile = COL_TILE // LANES
    dsa_scale = float(DSA_HEAD_DIM ** -0.5 * LOG2_E)
    idx_scale = float((IDX_HEADS * IDX_HEAD_DIM) ** -0.5)
    rope_q = (("rope", 1, dsa_scale),) * per_tile
    rope_k = (("rope", 1, 1.0),) * per_tile
    rope_idx = (("rope", 2, 1.0),) * per_tile
    plain = (("plain", None, 1.0),) * per_tile
    sigm = (("sigmoid", None, 1.0),) * per_tile
    small = (("rope", 0, 1.0), ("rope", 2, 1.0), ("rope", 2, 1.0), ("hilo", None, idx_scale))
    return (
        (rope_q, ((t("q_a"), t("q_i")),)),
        (rope_k, ((t("k_a"), t("v_a")),)),
        (rope_idx, ((t("q_i"), t("c_q")),)),
        ("rms", ((t("c_q"), t("conv_a")),)),
        (plain, ((t("conv_a"), t("gate")), (t("v_a"), t("small")))),
        (sigm, ((t("gate"), t("k_a")),)),
        (small, ((t("small"), t("total")),)),
    )


def _pick(n, pref):
    return pref if n % pref == 0 else n


def kernel(x, attn_norm, w_in, mla_q_norm, mla_kv_norm, mla_w_uq, mla_w_ukv, conv_w_dw, conv_b_dw,
           conv_ln_g, conv_ln_b, w_o_mla, w_o_dsa, w_o_conv, w_out, mlp_norm, w_up, w_down, final_norm):
    batch, seq_len, d_model = x.shape
    depth = w_in.shape[0]
    n_rows = batch * seq_len
    off = _in_layout(d_model)
    kinds = _tile_ranges(off)

    tab_mla, half_mla = _rope_lane_tables(seq_len, MLA_ROPE, (0,))
    tab_dsa, half_dsa = _rope_lane_tables(seq_len, DSA_ROT, (0,))
    tab_idx, half_idx = _rope_lane_tables(seq_len, IDX_ROT, (0, IDX_HEAD_DIM))
    tabs = (tab_mla, tab_dsa, tab_idx)
    halves = (half_mla, half_dsa, half_idx)

    mla_scale = float((MLA_NOPE + MLA_ROPE) ** -0.5 * LOG2_E)
    mla_cols = MLA_HEADS * 2 * LANES
    q_kind = (("plain", None, mla_scale), ("rope", 0, mla_scale)) * MLA_HEADS
    kv_kind = (("plain", None, 1.0),) * (mla_cols // LANES)
    everywhere = ((0, 1),)

    w_o_mla_b, w_o_dsa_b, w_o_conv_b, w_out_b, w_up_b, w_down_b = (
        w.astype(BF16) for w in (w_o_mla, w_o_dsa, w_o_conv, w_out, w_up, w_down))

    tm = _pick(seq_len, 512)
    xf = x.reshape(n_rows, d_model)
    for i in range(depth):
        ones = jnp.ones((off["total"],), F32)
        rms_gain = lax.dynamic_update_slice(ones, mla_q_norm[i].astype(F32), (off["c_q"],))
        rms_gain = lax.dynamic_update_slice(rms_gain, mla_kv_norm[i].astype(F32), (off["c_kv"],))
        big = _project(xf, _prep_w_in(w_in[i], d_model), gain=attn_norm[i], rms_gain=rms_gain.reshape(1, -1),
                       rope_tabs=tabs, rope_halves=halves, kinds=kinds, seq_len=seq_len,
                       tm=_pick(seq_len, 1024))

        q_mla = _project(big, _prep_w_uq(mla_w_uq[i]), a_col_block=off["c_q"] // MLA_Q_LORA,
                         rope_tabs=(tab_mla,), rope_halves=(half_mla,), kinds=((q_kind, everywhere),),
                         seq_len=seq_len, tm=tm, tn=mla_cols)
        kv_mla = _project(big, _prep_w_ukv(mla_w_ukv[i]), a_col_block=off["c_kv"] // MLA_KV_LORA,
                          kinds=((kv_kind, everywhere),), seq_len=seq_len, tm=tm, tn=mla_cols)
        o_mla = _mla_attention(q_mla, kv_mla, big, batch=batch, seq_len=seq_len,
                               kr_col_block=off["small"] // LANES, tq=_pick(seq_len, 1024), tk=512)
        o_dsa = _dsa_attention(big, batch=batch, seq_len=seq_len, off=off, tq=_pick(seq_len, 256), tk=512)

        gated_conv = _conv_branch(big, conv_w_dw[i], conv_b_dw[i], conv_ln_g[i], conv_ln_b[i], w_o_conv_b,
                                  layer=i, batch=batch, seq_len=seq_len, off=off, d_model=d_model,
                                  tt=_pick(seq_len, 256))
        xf = _merge_project(o_mla, o_dsa, big, gated_conv, xf, w_o_mla_b, w_o_dsa_b, w_out_b, layer=i,
                            gate_col_block=off["gate"] // d_model, tm=_pick(seq_len, 256))
        xf = _mlp(xf, mlp_norm[i], w_up_b, w_down_b, final_norm, layer=i,
                  final_norm=(i == depth - 1), tm=tm, tf=_pick(w_up.shape[2], 1024))
    return xf.reshape(batch, seq_len, d_model)
```
